```python
import math
import jax, jax.numpy as jnp
from jax import lax
import numpy as np

D_MODEL = 1024
BATCH = 8
SEQ = 4096
DEPTH = 2

CHUNK = 64
N_EVEN = (DEPTH + 1) // 2
N_ODD = DEPTH // 2
EPS = 1e-6

S5_WIDTH = D_MODEL
S5_GROUP = 16
S5_GROUPS = S5_WIDTH // S5_GROUP
S5_STATE = 64

RET_HEADS = 8
RET_DK = D_MODEL // (2 * RET_HEADS)
RET_DV = D_MODEL // RET_HEADS
RET_QK = RET_HEADS * RET_DK
RET_WIDTH = RET_HEADS * RET_DV
ROPE_BASE = 10000.0

AB_IN = 2 * S5_WIDTH + 2 * RET_QK + 2 * RET_WIDTH
AB_MIX = S5_WIDTH + RET_WIDTH

CONV_WIDTH = D_MODEL
CONV_K = 31
C_IN = 3 * CONV_WIDTH

kernel_name = 'hybrid_s5_retention_conformer_conv'


def rmsnorm(x, g):
    xf = x.astype(jnp.float32)
    y = xf * lax.rsqrt(jnp.mean(xf * xf, axis=-1, keepdims=True) + EPS)
    return (y * g.astype(jnp.float32)).astype(x.dtype)


def _complex_affine_combine(e1, e2):
    a1r, a1i, b1r, b1i = e1
    a2r, a2i, b2r, b2i = e2
    ar = a2r * a1r - a2i * a1i
    ai = a2r * a1i + a2i * a1r
    br = a2r * b1r - a2i * b1i + b2r
    bi = a2r * b1i + a2i * b1r + b2i
    return ar, ai, br, bi


def s5_mixer(u, a_re, a_im, log_dt, b_re, b_im, c_re, c_im, d_skip, glu_w, glu_b):
    f32 = jnp.float32
    bsz, seq, _ = u.shape
    n_chunks = seq // CHUNK
    uf = u.astype(f32)
    are, aim = a_re.astype(f32), a_im.astype(f32)
    dt = jnp.exp(log_dt.astype(f32))[:, None]
    mag = jnp.exp(are * dt)
    lb_re = mag * jnp.cos(aim * dt)
    lb_im = mag * jnp.sin(aim * dt)
    nr, ni = lb_re - 1.0, lb_im
    den = are * are + aim * aim
    coef_re = (nr * are + ni * aim) / den
    coef_im = (ni * are - nr * aim) / den
    bre, bim = b_re.astype(f32), b_im.astype(f32)
    bb_re = coef_re[..., None] * bre - coef_im[..., None] * bim
    bb_im = coef_re[..., None] * bim + coef_im[..., None] * bre
    cre, cim = c_re.astype(f32), c_im.astype(f32)
    dsk = d_skip.astype(f32).reshape(S5_GROUPS, S5_GROUP)
    a_r = jnp.broadcast_to(lb_re, (CHUNK, 1, S5_GROUPS, S5_STATE))
    a_i = jnp.broadcast_to(lb_im, (CHUNK, 1, S5_GROUPS, S5_STATE))

    u_blk = uf.reshape(bsz, n_chunks, CHUNK, S5_GROUPS, S5_GROUP).transpose(1, 2, 0, 3, 4)

    def chunk_step(carry, u_c):
        s_re, s_im = carry
        bu_re = jnp.einsum('tbgh,gnh->tbgn', u_c, bb_re)
        bu_im = jnp.einsum('tbgh,gnh->tbgn', u_c, bb_im)
        pr, pim, zr, zi = lax.associative_scan(
            _complex_affine_combine, (a_r, a_i, bu_re, bu_im), axis=0)
        xr = zr + pr * s_re - pim * s_im
        xi = zi + pr * s_im + pim * s_re
        y = (jnp.einsum('tbgn,ghn->tbgh', xr, cre)
             - jnp.einsum('tbgn,ghn->tbgh', xi, cim)
             + dsk * u_c)
        return (xr[-1], xi[-1]), y

    init = (jnp.zeros((bsz, S5_GROUPS, S5_STATE), f32),
            jnp.zeros((bsz, S5_GROUPS, S5_STATE), f32))
    _, y = lax.scan(chunk_step, init, u_blk)
    y = y.transpose(2, 0, 1, 3, 4).reshape(bsz, seq, S5_WIDTH)
    y = jax.nn.gelu(y)
    y = y * jax.nn.sigmoid(y @ glu_w.astype(f32) + glu_b.astype(f32))
    return y.astype(u.dtype)


def rope(x, positions):
    half = x.shape[-1] // 2
    freqs = ROPE_BASE ** (-jnp.arange(half, dtype=jnp.float32) / half)
    ang = positions.astype(jnp.float32)[:, None] * freqs[None, :]
    cos = jnp.cos(ang)[None, :, None, :]
    sin = jnp.sin(ang)[None, :, None, :]
    x1, x2 = x[..., :half], x[..., half:]
    return jnp.concatenate([x1 * cos - x2 * sin, x1 * sin + x2 * cos], axis=-1)


def retention(q, k, v):
    f32 = jnp.float32
    bsz, seq = q.shape[0], q.shape[1]
    n_chunks = seq // CHUNK
    positions = jnp.arange(seq, dtype=jnp.int32)
    q = rope(q.astype(f32), positions) * (RET_DK ** -0.5)
    k = rope(k.astype(f32), positions)
    v = v.astype(f32)
    log_g = jnp.log(1.0 - 2.0 ** (-5.0 - jnp.arange(RET_HEADS, dtype=f32)))
    idx = jnp.arange(CHUNK, dtype=f32)
    intra = jnp.exp(log_g[:, None, None] * jnp.abs(idx[:, None] - idx[None, :]))
    qc = q.reshape(bsz, n_chunks, CHUNK, RET_HEADS, RET_DK)
    kc = k.reshape(bsz, n_chunks, CHUNK, RET_HEADS, RET_DK)
    vc = v.reshape(bsz, n_chunks, CHUNK, RET_HEADS, RET_DV)
    scores = jnp.einsum('bnihd,bnjhd->bnhij', qc, kc) * intra
    intra_out = jnp.einsum('bnhij,bnjhe->bnihe', scores, vc)
    k_dec = jnp.exp(log_g[None, :] * (CHUNK - 1 - idx)[:, None])
    kv = jnp.einsum('bnjhd,jh,bnjhe->bnhde', kc, k_dec, vc)
    chunk_decay = jnp.exp(log_g * CHUNK)[:, None, None]

    def state_step(s, kv_n):
        return chunk_decay * s + kv_n, s

    _, s_prev = lax.scan(state_step, jnp.zeros((bsz, RET_HEADS, RET_DK, RET_DV), f32),
                         kv.transpose(1, 0, 2, 3, 4))
    s_prev = s_prev.transpose(1, 0, 2, 3, 4)
    q_dec = jnp.exp(log_g[None, :] * (idx + 1.0)[:, None])
    cross = jnp.einsum('bnihd,ih,bnhde->bnihe', qc, q_dec, s_prev)
    out = (intra_out + cross).reshape(bsz, seq, RET_HEADS, RET_DV)
    mu = jnp.mean(out, axis=-1, keepdims=True)
    var = jnp.mean(jnp.square(out - mu), axis=-1, keepdims=True)
    out = (out - mu) * lax.rsqrt(var + EPS)
    return out.reshape(bsz, seq, RET_WIDTH)


def mixer_ab(h, w_in, a_re, a_im, log_dt, b_re, b_im, c_re, c_im, d_skip,
             glu_w, glu_b, w_out):
    bsz, seq, _ = h.shape
    proj = h @ w_in
    cuts = np.cumsum([S5_WIDTH, S5_WIDTH, RET_QK, RET_QK, RET_WIDTH]).tolist()
    u_s5, g_s5, q, k, v, g_ret = jnp.split(proj, cuts, axis=-1)
    y_s5 = s5_mixer(u_s5, a_re, a_im, log_dt, b_re, b_im, c_re, c_im, d_skip, glu_w, glu_b)
    y_s5 = y_s5 * jax.nn.silu(g_s5)
    q = q.reshape(bsz, seq, RET_HEADS, RET_DK)
    k = k.reshape(bsz, seq, RET_HEADS, RET_DK)
    v = v.reshape(bsz, seq, RET_HEADS, RET_DV)
    y_ret = retention(q, k, v).astype(h.dtype) * jax.nn.silu(g_ret)
    return jnp.concatenate([y_s5, y_ret], axis=-1) @ w_out


def mixer_conv(h, w_in, conv_w, conv_b, ln_g, ln_b, w_out):
    proj = h @ w_in
    a, b, g = jnp.split(proj, 3, axis=-1)
    u = a * jax.nn.sigmoid(b)
    u = lax.conv_general_dilated(
        u, conv_w[:, None, :].astype(u.dtype), window_strides=(1,),
        padding=[(CONV_K - 1, 0)], dimension_numbers=('NWC', 'WIO', 'NWC'),
        feature_group_count=CONV_WIDTH) + conv_b
    uf = u.astype(jnp.float32)
    mu = jnp.mean(uf, axis=-1, keepdims=True)
    var = jnp.mean(jnp.square(uf - mu), axis=-1, keepdims=True)
    uf = (uf - mu) * lax.rsqrt(var + EPS) * ln_g.astype(jnp.float32) + ln_b.astype(jnp.float32)
    u = jax.nn.silu(uf).astype(h.dtype) * jax.nn.silu(g)
    return u @ w_out


def setup_inputs(seed: int = 0) -> dict:
    key = jax.random.key(seed)
    ks = jax.random.split(key, 24)
    f32 = jnp.float32
    nrm = lambda k, shape, s: jax.random.normal(k, shape, f32) * s
    G, N, H = S5_GROUPS, S5_STATE, S5_GROUP
    a_re = -0.5 + nrm(ks[4], (N_EVEN, G, N), 0.01)
    a_im = math.pi * jnp.arange(N, dtype=f32)[None, None, :] + nrm(ks[5], (N_EVEN, G, N), 0.01)
    log_dt = jax.random.uniform(ks[6], (N_EVEN, G), f32, math.log(1e-3), math.log(1e-1))
    return {
        'x': nrm(ks[0], (BATCH, SEQ, D_MODEL), 1.0),
        'norm_g': 1.0 + nrm(ks[1], (DEPTH, D_MODEL), 0.01),
        'final_g': 1.0 + nrm(ks[2], (D_MODEL,), 0.01),
        'w_in_ab': nrm(ks[3], (N_EVEN, D_MODEL, AB_IN), D_MODEL ** -0.5),
        's5_a_re': a_re,
        's5_a_im': a_im,
        's5_log_dt': log_dt,
        's5_b_re': nrm(ks[7], (N_EVEN, G, N, H), (2.0 * H) ** -0.5),
        's5_b_im': nrm(ks[8], (N_EVEN, G, N, H), (2.0 * H) ** -0.5),
        's5_c_re': nrm(ks[9], (N_EVEN, G, H, N), (2.0 * N) ** -0.5),
        's5_c_im': nrm(ks[10], (N_EVEN, G, H, N), (2.0 * N) ** -0.5),
        's5_d': nrm(ks[11], (N_EVEN, S5_WIDTH), 1.0),
        's5_glu_w': nrm(ks[12], (N_EVEN, S5_WIDTH, S5_WIDTH), S5_WIDTH ** -0.5),
        's5_glu_b': nrm(ks[13], (N_EVEN, S5_WIDTH), 0.01),
        'w_out_ab': nrm(ks[14], (N_EVEN, AB_MIX, D_MODEL), AB_MIX ** -0.5),
        'w_in_c': nrm(ks[15], (N_ODD, D_MODEL, C_IN), D_MODEL ** -0.5),
        'conv_w': nrm(ks[16], (N_ODD, CONV_K, CONV_WIDTH), CONV_K ** -0.5),
        'conv_b': nrm(ks[17], (N_ODD, CONV_WIDTH), 0.01),
        'conv_ln_g': 1.0 + nrm(ks[18], (N_ODD, CONV_WIDTH), 0.01),
        'conv_ln_b': nrm(ks[19], (N_ODD, CONV_WIDTH), 0.01),
        'w_out_c': nrm(ks[20], (N_ODD, CONV_WIDTH, D_MODEL), CONV_WIDTH ** -0.5),
    }


def reference(x, norm_g, final_g, w_in_ab, s5_a_re, s5_a_im, s5_log_dt, s5_b_re,
              s5_b_im, s5_c_re, s5_c_im, s5_d, s5_glu_w, s5_glu_b, w_out_ab,
              w_in_c, conv_w, conv_b, conv_ln_g, conv_ln_b, w_out_c):
    for layer in range(DEPTH):
        h = rmsnorm(x, norm_g[layer])
        i = layer // 2
        if layer % 2 == 0:
            x = x + mixer_ab(h, w_in_ab[i], s5_a_re[i], s5_a_im[i], s5_log_dt[i],
                             s5_b_re[i], s5_b_im[i], s5_c_re[i], s5_c_im[i], s5_d[i],
                             s5_glu_w[i], s5_glu_b[i], w_out_ab[i])
        else:
            x = x + mixer_conv(h, w_in_c[i], conv_w[i], conv_b[i], conv_ln_g[i],
                               conv_ln_b[i], w_out_c[i])
    return rmsnorm(x, final_g)
```

```python
import functools
import math

import jax
import jax.numpy as jnp
from jax import lax
from jax.experimental import pallas as pl
from jax.experimental.pallas import tpu as pltpu

F32 = jnp.float32
BF16 = jnp.bfloat16

EPS = 1e-6
CHUNK = 64
S5_GROUP = 16
S5_STATE = 64
S5_R = 16
S5_BLK = S5_R * S5_GROUP
RET_HEADS = 8
RET_DK = 64
RET_DV = 128
ROPE_BASE = 10000.0
CONV_K = 31
CONV_HALO = 32

LANES = 128
VMEM_LIMIT_BYTES = 56 * 1024 * 1024

_NT = (((1,), (1,)), ((), ()))
_TN = (((0,), (0,)), ((), ()))


def _cparams(sem):
    return pltpu.CompilerParams(dimension_semantics=sem, vmem_limit_bytes=VMEM_LIMIT_BYTES)


def _rms(x, g):
    return x * lax.rsqrt(jnp.mean(x * x, axis=-1, keepdims=True) + EPS) * g


def _proj0_kernel(x_ref, g_ref, w_ref, *out_refs):
    hn = _rms(x_ref[...], g_ref[...]).astype(BF16)
    off = 0
    for ref in out_refs:
        wd = ref.shape[-1]
        ref[...] = jnp.dot(hn, w_ref[:, off:off + wd], preferred_element_type=F32).astype(ref.dtype)
        off += wd


def _proj0(x2, g, w, widths, tm):
    rows, d = x2.shape
    out_shape = [jax.ShapeDtypeStruct((rows, wd), BF16) for wd in widths]
    return pl.pallas_call(
        _proj0_kernel,
        grid=(rows // tm,),
        in_specs=[
            pl.BlockSpec((tm, d), lambda i: (i, 0)),
            pl.BlockSpec((1, d), lambda i: (0, 0)),
            pl.BlockSpec(w.shape, lambda i: (0, 0)),
        ],
        out_specs=[pl.BlockSpec((tm, wd), lambda i: (i, 0)) for wd in widths],
        out_shape=out_shape,
        compiler_params=_cparams(("parallel",)),
        name="proj0",
    )(x2, g, w)


def _s5_prep_kernel(are_ref, aim_ref, ldt_ref, bre_ref, bim_ref, cre_ref, cim_ref, dsk_ref,
                    mt_ref, bsre_ref, bsim_ref, ccat_ref, a16_ref):
    g = pl.program_id(0)
    are = are_ref[0]
    aim = aim_ref[0]
    dt = jnp.exp(ldt_ref[0])
    bre, bim = bre_ref[0], bim_ref[0]
    cre, cim = cre_ref[0], cim_ref[0]

    def lpow(e):
        mag = jnp.exp(are * dt * e)
        ang = aim * dt * e
        return mag * jnp.cos(ang), mag * jnp.sin(ang)

    one = jnp.ones((1, 1), F32)
    lb_re, lb_im = lpow(one)
    nr, ni = lb_re - 1.0, lb_im
    den = are * are + aim * aim
    coef_re = (nr * are + ni * aim) / den
    coef_im = (ni * are - nr * aim) / den
    bb_re = coef_re * bre - coef_im * bim
    bb_im = coef_re * bim + coef_im * bre

    p = lax.broadcasted_iota(jnp.int32, (S5_R, 1), 0).astype(F32)
    lane = lax.broadcasted_iota(jnp.int32, (1, LANES), 1)
    mine = (lane // S5_STATE) == (g % 2)
    low = lane < S5_STATE

    def outer(t_re, t_im, m_re, m_im):
        re = t_re[:, None, :] * m_re[None, :, :] - t_im[:, None, :] * m_im[None, :, :]
        im = t_re[:, None, :] * m_im[None, :, :] + t_im[:, None, :] * m_re[None, :, :]
        return re.reshape(S5_BLK, LANES), im.reshape(S5_BLK, LANES)

    half = float(S5_R // 2)
    e_re, e_im = lpow(p - half)
    f_re, f_im = lpow(half - p)
    p_re, p_im = outer(e_re, e_im, cre, cim)
    q_re, q_im = outer(f_re, f_im, bb_re, bb_im)
    q_re = jnp.where(low, q_re, 0.0)
    q_im = jnp.where(low, q_im, 0.0)
    mt = (lax.dot_general(p_re, q_re, _NT, precision=lax.Precision.HIGHEST, preferred_element_type=F32)
          - lax.dot_general(p_im, q_im, _NT, precision=lax.Precision.HIGHEST, preferred_element_type=F32))
    row = lax.broadcasted_iota(jnp.int32, (S5_BLK, S5_BLK), 0)
    col = lax.broadcasted_iota(jnp.int32, (S5_BLK, S5_BLK), 1)
    mt = jnp.where(row // S5_GROUP >= col // S5_GROUP, mt, 0.0)
    mt = mt + jnp.where(row == col, dsk_ref[0], 0.0)
    mt_ref[0] = mt.astype(mt_ref.dtype)

    g_re, g_im = lpow(float(S5_R - 1) - p)
    bs_re, bs_im = outer(g_re, g_im, bb_re, bb_im)
    bsre_ref[0] = jnp.where(mine, bs_re, 0.0).astype(bsre_ref.dtype)
    bsim_ref[0] = jnp.where(mine, bs_im, 0.0).astype(bsim_ref.dtype)

    h_re, h_im = lpow(p + 1.0)
    cs_re, cs_im = outer(h_re, h_im, cre, cim)
    ccat_ref[0, :, 0:LANES] = jnp.where(mine, cs_re, 0.0).astype(ccat_ref.dtype)
    ccat_ref[0, :, LANES:2 * LANES] = jnp.where(mine, -cs_im, 0.0).astype(ccat_ref.dtype)

    a_re, a_im = lpow(one * float(S5_R))
    rows8 = lax.broadcasted_iota(jnp.int32, (8, LANES), 0)
    a16_ref[0] = jnp.where(mine, jnp.where(rows8 == 0, a_re, jnp.where(rows8 == 1, a_im, 0.0)), 0.0)


def _s5_prep(are2, aim2, ldt, bre2, bim2, cre2, cim2, dsk_t):
    G = are2.shape[0]
    spec3 = lambda shape: pl.BlockSpec((1,) + shape, lambda g: (g, 0, 0))
    return pl.pallas_call(
        _s5_prep_kernel,
        grid=(G,),
        in_specs=[spec3((1, LANES)), spec3((1, LANES)), spec3((1, 1)),
                  spec3((S5_GROUP, LANES)), spec3((S5_GROUP, LANES)),
                  spec3((S5_GROUP, LANES)), spec3((S5_GROUP, LANES)), spec3((S5_BLK, 1))],
        out_specs=[spec3((S5_BLK, S5_BLK)), spec3((S5_BLK, LANES)), spec3((S5_BLK, LANES)),
                   spec3((S5_BLK, 2 * LANES)), spec3((8, LANES))],
        out_shape=[jax.ShapeDtypeStruct((G, S5_BLK, S5_BLK), BF16),
                   jax.ShapeDtypeStruct((G, S5_BLK, LANES), BF16),
                   jax.ShapeDtypeStruct((G, S5_BLK, LANES), BF16),
                   jax.ShapeDtypeStruct((G, S5_BLK, 2 * LANES), BF16),
                   jax.ShapeDtypeStruct((G, 8, LANES), F32)],
        compiler_params=_cparams(("parallel",)),
        name="s5_prep",
    )(are2, aim2, ldt, bre2, bim2, cre2, cim2, dsk_t)


def _s5_core_kernel(at_ref, mt_ref, bsre_ref, bsim_ref, ccat_ref, a16_ref, yt_ref,
                    zre_ref, zim_ref, xre_ref, xim_ref, *, nb, nj):
    zre = None
    for i in range(2):
        at = at_ref[i]
        r = lax.dot_general(at, bsre_ref[i], _TN, preferred_element_type=F32)
        m = lax.dot_general(at, bsim_ref[i], _TN, preferred_element_type=F32)
        zre = r if zre is None else zre + r
        zim = m if i == 0 else zim + m
    zre_ref[...] = zre
    zim_ref[...] = zim
    a_re = jnp.broadcast_to(a16_ref[0, 0:1, :] + a16_ref[1, 0:1, :], (nb, LANES))
    a_im = jnp.broadcast_to(a16_ref[0, 1:2, :] + a16_ref[1, 1:2, :], (nb, LANES))

    def step(j, carry):
        xr, xi = carry
        idx = pl.ds(j, nb, stride=nj)
        xre_ref[idx, :] = xr
        xim_ref[idx, :] = xi
        zr = zre_ref[idx, :]
        zi = zim_ref[idx, :]
        return a_re * xr - a_im * xi + zr, a_re * xi + a_im * xr + zi

    zero = jnp.zeros((nb, LANES), F32)
    lax.fori_loop(0, nj, step, (zero, zero))
    xcat = jnp.concatenate([xre_ref[...], xim_ref[...]], axis=1).astype(BF16)
    for i in range(2):
        yt = (jnp.dot(mt_ref[i], at_ref[i], preferred_element_type=F32)
              + lax.dot_general(ccat_ref[i], xcat, _NT, preferred_element_type=F32))
        yt_ref[i] = yt.astype(yt_ref.dtype)


def _s5_core(at, mt, bsre, bsim, ccat, a16, nb, nj):
    G, _, cols = at.shape
    pair = lambda shape: pl.BlockSpec((2,) + shape, lambda p: (p, 0, 0))
    return pl.pallas_call(
        functools.partial(_s5_core_kernel, nb=nb, nj=nj),
        grid=(G // 2,),
        in_specs=[pair((S5_BLK, cols)), pair((S5_BLK, S5_BLK)), pair((S5_BLK, LANES)),
                  pair((S5_BLK, LANES)), pair((S5_BLK, 2 * LANES)), pair((8, LANES))],
        out_specs=pair((S5_BLK, cols)),
        out_shape=jax.ShapeDtypeStruct((G, S5_BLK, cols), BF16),
        scratch_shapes=[pltpu.VMEM((cols, LANES), F32)] * 4,
        compiler_params=_cparams(("parallel",)),
        name="s5_core",
    )(at, mt, bsre, bsim, ccat, a16)


def _ret_kernel(q_ref, k_ref, v_ref, cos_ref, sin_ref, intra_ref, qdec_ref, kdec_ref, cdec_ref,
                o_ref, s_ref, qs_ref, ks_ref, *, n_chunks):
    @pl.when(pl.program_id(1) == 0)
    def _():
        s_ref[...] = jnp.zeros(s_ref.shape, s_ref.dtype)

    cos = cos_ref[...]
    sin = sin_ref[...]
    width = q_ref.shape[-1]
    lane = lax.broadcasted_iota(jnp.int32, cos.shape, 1)
    first = (lane % RET_DK) < (RET_DK // 2)

    def rope(x):
        partner = jnp.where(first, pltpu.roll(x, width - RET_DK // 2, 1), pltpu.roll(x, RET_DK // 2, 1))
        return x * cos + partner * sin

    qs_ref[...] = (rope(q_ref[...].astype(F32)) * (RET_DK ** -0.5)).astype(BF16)
    ks_ref[...] = rope(k_ref[...].astype(F32)).astype(BF16)

    lane_p = lax.broadcasted_iota(jnp.int32, (CHUNK, LANES), 1)
    for c in range(n_chunks):
        rows = pl.ds(c * CHUNK, CHUNK)
        for h in range(RET_HEADS):
            pr = pl.ds((h // 2) * LANES, LANES)
            qp = qs_ref[rows, pr]
            kp = ks_ref[rows, pr]
            km = jnp.where((lane_p // RET_DK) == (h % 2), kp, jnp.zeros_like(kp))
            vh = v_ref[rows, pl.ds(h * RET_DV, RET_DV)]
            sc = lax.dot_general(qp, km, _NT, preferred_element_type=F32) * intra_ref[h]
            o = jnp.dot(sc.astype(BF16), vh, preferred_element_type=F32)
            s_prev = s_ref[h]
            o = o + qdec_ref[h] * jnp.dot(qp, s_prev.astype(BF16), preferred_element_type=F32)
            vd = (vh.astype(F32) * kdec_ref[h]).astype(BF16)
            kv = lax.dot_general(km, vd, _TN, preferred_element_type=F32)
            s_ref[h] = cdec_ref[h] * s_prev + kv
            mu = jnp.mean(o, axis=-1, keepdims=True)
            var = jnp.mean(jnp.square(o - mu), axis=-1, keepdims=True)
            o_ref[rows, pl.ds(h * RET_DV, RET_DV)] = ((o - mu) * lax.rsqrt(var + EPS)).astype(o_ref.dtype)


def _retention(q, k, v, cos, sin, intra, qdec, kdec, cdec, nb, seq, tb):
    rows = q.shape[0]
    nt = seq // tb
    row_blk = lambda wd: pl.BlockSpec((tb, wd), lambda b, t: (b * nt + t, 0))
    pos_blk = lambda wd: pl.BlockSpec((tb, wd), lambda b, t: (t, 0))
    full = lambda a: pl.BlockSpec(a.shape, lambda b, t: (0,) * a.ndim)
    qk_w, v_w = q.shape[-1], v.shape[-1]
    return pl.pallas_call(
        functools.partial(_ret_kernel, n_chunks=tb // CHUNK),
        grid=(nb, nt),
        in_specs=[row_blk(qk_w), row_blk(qk_w), row_blk(v_w), pos_blk(qk_w), pos_blk(qk_w),
                  full(intra), full(qdec), full(kdec), full(cdec)],
        out_specs=row_blk(v_w),
        out_shape=jax.ShapeDtypeStruct((rows, v_w), BF16),
        scratch_shapes=[pltpu.VMEM((RET_HEADS, LANES, RET_DV), F32),
                        pltpu.VMEM((tb, qk_w), BF16), pltpu.VMEM((tb, qk_w), BF16)],
        compiler_params=_cparams(("parallel", "arbitrary")),
        name="retention",
    )(q, k, v, cos, sin, intra, qdec, kdec, cdec)


def _merge0_kernel(x_ref, y5_ref, g5_ref, yr_ref, gr_ref, gw_ref, gb_ref, wo5_ref, wor_ref, o_ref):
    y = jax.nn.gelu(y5_ref[...].astype(F32))
    gate = jnp.dot(y.astype(BF16), gw_ref[...], preferred_element_type=F32) + gb_ref[...]
    a = y * jax.nn.sigmoid(gate) * jax.nn.silu(g5_ref[...].astype(F32))
    r = yr_ref[...].astype(F32) * jax.nn.silu(gr_ref[...].astype(F32))
    o_ref[...] = (x_ref[...]
                  + jnp.dot(a.astype(BF16), wo5_ref[...], preferred_element_type=F32)
                  + jnp.dot(r.astype(BF16), wor_ref[...], preferred_element_type=F32))


def _merge0(x2, y5, g5, yr, gr, gw, gb, wo5, wor, tm):
    rows, d = x2.shape
    row_blk = lambda wd: pl.BlockSpec((tm, wd), lambda i: (i, 0))
    full = lambda a: pl.BlockSpec(a.shape, lambda i: (0,) * a.ndim)
    return pl.pallas_call(
        _merge0_kernel,
        grid=(rows // tm,),
        in_specs=[row_blk(d), row_blk(y5.shape[1]), row_blk(g5.shape[1]), row_blk(yr.shape[1]),
                  row_blk(gr.shape[1]), full(gw), full(gb), full(wo5), full(wor)],
        out_specs=row_blk(d),
        out_shape=jax.ShapeDtypeStruct((rows, d), F32),
        compiler_params=_cparams(("parallel",)),
        name="merge0",
    )(x2, y5, g5, yr, gr, gw, gb, wo5, wor)


def _layer1_kernel(x_ref, g_ref, win_ref, cw_ref, cb_ref, lg_ref, lb_ref, wout_ref, fg_ref, o_ref,
                   ubuf_ref, *, tm, width):
    x = x_ref[...]
    hn = _rms(x, g_ref[...]).astype(BF16)
    a = jnp.dot(hn, win_ref[:, 0:width], preferred_element_type=F32)
    b = jnp.dot(hn, win_ref[:, width:2 * width], preferred_element_type=F32)
    gate = jnp.dot(hn, win_ref[:, 2 * width:3 * width], preferred_element_type=F32)

    @pl.when(pl.program_id(1) == 0)
    def _():
        ubuf_ref[0:CONV_HALO, :] = jnp.zeros((CONV_HALO, width), F32)

    ubuf_ref[CONV_HALO:CONV_HALO + tm, :] = a * jax.nn.sigmoid(b)
    base = CONV_HALO - (CONV_K - 1)
    acc = jnp.zeros((tm, width), F32) + cb_ref[...]
    for kk in range(CONV_K):
        acc = acc + ubuf_ref[base + kk:base + kk + tm, :] * cw_ref[kk:kk + 1, :]
    ubuf_ref[0:CONV_HALO, :] = ubuf_ref[tm:tm + CONV_HALO, :]

    mu = jnp.mean(acc, axis=-1, keepdims=True)
    var = jnp.mean(jnp.square(acc - mu), axis=-1, keepdims=True)
    uf = (acc - mu) * lax.rsqrt(var + EPS) * lg_ref[...] + lb_ref[...]
    u = jax.nn.silu(uf) * jax.nn.silu(gate)
    x1 = x + jnp.dot(u.astype(BF16), wout_ref[...], preferred_element_type=F32)
    o_ref[...] = _rms(x1, fg_ref[...])


def _layer1(x2, g, win, cw, cb, lg, lb, wout, fg, nb, seq, tm):
    rows, d = x2.shape
    width = wout.shape[0]
    nt = seq // tm
    row_blk = pl.BlockSpec((tm, d), lambda b, t: (b * nt + t, 0))
    full = lambda a: pl.BlockSpec(a.shape, lambda b, t: (0,) * a.ndim)
    return pl.pallas_call(
        functools.partial(_layer1_kernel, tm=tm, width=width),
        grid=(nb, nt),
        in_specs=[row_blk, full(g), full(win), full(cw), full(cb), full(lg), full(lb), full(wout), full(fg)],
        out_specs=row_blk,
        out_shape=jax.ShapeDtypeStruct((rows, d), F32),
        scratch_shapes=[pltpu.VMEM((CONV_HALO + tm, width), F32)],
        compiler_params=_cparams(("parallel", "arbitrary")),
        name="layer1",
    )(x2, g, win, cw, cb, lg, lb, wout, fg)


def _pick_tile(n, pref):
    t = min(n, pref)
    while n % t:
        t //= 2
    return t


def kernel(x, norm_g, final_g, w_in_ab, s5_a_re, s5_a_im, s5_log_dt, s5_b_re, s5_b_im, s5_c_re, s5_c_im,
           s5_d, s5_glu_w, s5_glu_b, w_out_ab, w_in_c, conv_w, conv_b, conv_ln_g, conv_ln_b, w_out_c):
    nb, seq, d = x.shape
    rows = nb * seq
    s5_w = s5_glu_w.shape[-1]
    G = s5_w // S5_GROUP
    ret_qk = RET_HEADS * RET_DK
    ret_w = RET_HEADS * RET_DV
    nj = seq // S5_R
    x2 = x.reshape(rows, d)

    widths = (s5_w, s5_w, ret_qk, ret_qk, ret_w, ret_w)
    u5, g5, q, k, v, gr = _proj0(x2, norm_g[0:1], w_in_ab[0].astype(BF16), widths, _pick_tile(rows, 512))

    dup = lambda a: jnp.concatenate([a, a], axis=-1)
    are2 = dup(s5_a_re[0])[:, None, :]
    aim2 = dup(s5_a_im[0])[:, None, :]
    ldt = s5_log_dt[0][:, None, None]
    bre2 = dup(jnp.swapaxes(s5_b_re[0], 1, 2))
    bim2 = dup(jnp.swapaxes(s5_b_im[0], 1, 2))
    cre2 = dup(s5_c_re[0])
    cim2 = dup(s5_c_im[0])
    dsk_t = jnp.tile(s5_d[0].reshape(G, 1, S5_GROUP), (1, S5_R, 1)).reshape(G, S5_BLK, 1)
    mt, bsre, bsim, ccat, a16 = _s5_prep(are2, aim2, ldt, bre2, bim2, cre2, cim2, dsk_t)

    at = u5.reshape(nb, nj, S5_R, G, S5_GROUP).transpose(3, 2, 4, 0, 1).reshape(G, S5_BLK, nb * nj)
    yt = _s5_core(at, mt, bsre, bsim, ccat, a16, nb, nj)
    y5 = yt.reshape(G, S5_R, S5_GROUP, nb, nj).transpose(3, 4, 1, 0, 2).reshape(rows, s5_w)

    pos = jnp.arange(seq, dtype=F32)[:, None]
    half = RET_DK // 2
    freqs = ROPE_BASE ** (-jnp.arange(half, dtype=F32) / half)
    ang = pos * freqs[None, :]
    cos_h = jnp.concatenate([jnp.cos(ang), jnp.cos(ang)], axis=-1)
    sin_h = jnp.concatenate([-jnp.sin(ang), jnp.sin(ang)], axis=-1)
    cos_t = jnp.tile(cos_h, (1, RET_HEADS))
    sin_t = jnp.tile(sin_h, (1, RET_HEADS))
    log_g = jnp.log(1.0 - 2.0 ** (-5.0 - jnp.arange(RET_HEADS, dtype=F32)))
    idx = jnp.arange(CHUNK, dtype=F32)
    intra = jnp.exp(log_g[:, None, None] * jnp.abs(idx[:, None] - idx[None, :]))
    qdec = jnp.exp(log_g[:, None] * (idx + 1.0)[None, :])[:, :, None]
    kdec = jnp.exp(log_g[:, None] * (CHUNK - 1 - idx)[None, :])[:, :, None]
    cdec = jnp.broadcast_to(jnp.exp(log_g * CHUNK)[:, None, None], (RET_HEADS, 1, RET_DV))
    yr = _retention(q, k, v, cos_t, sin_t, intra, qdec, kdec, cdec, nb, seq, _pick_tile(seq, 512))

    wo = w_out_ab[0].astype(BF16)
    x2 = _merge0(x2, y5, g5, yr, gr, s5_glu_w[0].astype(BF16), s5_glu_b[0:1], wo[:s5_w], wo[s5_w:],
                 _pick_tile(rows, 512))

    out = _layer1(x2, norm_g[1:2], w_in_c[0].astype(BF16), conv_w[0], conv_b[0:1], conv_ln_g[0:1],
                  conv_ln_b[0:1], w_out_c[0].astype(BF16), final_g[None, :], nb, seq, _pick_tile(seq, 512))
    return out.reshape(nb, seq, d)
```

```python
import functools
import math

import jax
import jax.numpy as jnp
from jax import lax
from jax.experimental import pallas as pl
from jax.experimental.pallas import tpu as pltpu

F32 = jnp.float32
BF16 = jnp.bfloat16

EPS = 1e-6
CHUNK = 64
S5_GROUP = 16
S5_STATE = 64
S5_R = 16
S5_BLK = S5_R * S5_GROUP
RET_HEADS = 8
RET_DK = 64
RET_DV = 128
ROPE_BASE = 10000.0
CONV_K = 31
CONV_HALO = 32

LANES = 128
VMEM_LIMIT_BYTES = 56 * 1024 * 1024

_NT = (((1,), (1,)), ((), ()))
_TN = (((0,), (0,)), ((), ()))


def _cparams(sem):
    return pltpu.CompilerParams(dimension_semantics=sem, vmem_limit_bytes=VMEM_LIMIT_BYTES)


def _rms(x, g):
    return x * lax.rsqrt(jnp.mean(x * x, axis=-1, keepdims=True) + EPS) * g


def _proj0_kernel(x_ref, g_ref, w_ref, *out_refs):
    hn = _rms(x_ref[...], g_ref[...]).astype(BF16)
    off = 0
    for ref in out_refs:
        wd = ref.shape[-1]
        ref[...] = jnp.dot(hn, w_ref[:, off:off + wd], preferred_element_type=F32).astype(ref.dtype)
        off += wd


def _proj0(x2, g, w, widths, tm):
    rows, d = x2.shape
    out_shape = [jax.ShapeDtypeStruct((rows, wd), BF16) for wd in widths]
    return pl.pallas_call(
        _proj0_kernel,
        grid=(rows // tm,),
        in_specs=[
            pl.BlockSpec((tm, d), lambda i: (i, 0)),
            pl.BlockSpec((1, d), lambda i: (0, 0)),
            pl.BlockSpec(w.shape, lambda i: (0, 0)),
        ],
        out_specs=[pl.BlockSpec((tm, wd), lambda i: (i, 0)) for wd in widths],
        out_shape=out_shape,
        compiler_params=_cparams(("parallel",)),
        name="proj0",
    )(x2, g, w)


def _proj_u_kernel(x_ref, g_ref, wt_ref, at_ref, g5t_ref, *, sg, d, s5_w):
    nj = x_ref.shape[1]
    for sl in range(sg):
        hn = _rms(x_ref[0, :, sl * d:(sl + 1) * d], g_ref[...]).astype(BF16)
        ut = lax.dot_general(wt_ref[...], hn, _NT, preferred_element_type=F32)
        at_ref[0, :, sl * S5_GROUP:(sl + 1) * S5_GROUP, :] = (
            ut[:s5_w].reshape(s5_w // S5_GROUP, S5_GROUP, nj).astype(at_ref.dtype))
        g5t_ref[0, sl] = ut[s5_w:].astype(g5t_ref.dtype)


def _proj_u(x3, g, wt, s5_w, sg):
    nb, nj, _ = x3.shape
    d = wt.shape[1]
    G = s5_w // S5_GROUP
    return pl.pallas_call(
        functools.partial(_proj_u_kernel, sg=sg, d=d, s5_w=s5_w),
        grid=(nb, S5_R // sg),
        in_specs=[
            pl.BlockSpec((1, nj, sg * d), lambda b, s: (b, 0, s)),
            pl.BlockSpec((1, d), lambda b, s: (0, 0)),
            pl.BlockSpec(wt.shape, lambda b, s: (0, 0)),
        ],
        out_specs=[
            pl.BlockSpec((1, G, sg * S5_GROUP, nj), lambda b, s: (b, 0, s, 0)),
            pl.BlockSpec((1, sg, s5_w, nj), lambda b, s: (b, s, 0, 0)),
        ],
        out_shape=[jax.ShapeDtypeStruct((nb, G, S5_BLK, nj), BF16),
                   jax.ShapeDtypeStruct((nb, S5_R, s5_w, nj), BF16)],
        compiler_params=_cparams(("parallel", "parallel")),
        name="proj_u",
    )(x3, g, wt)


def _s5_prep_kernel(are_ref, aim_ref, ldt_ref, bre_ref, bim_ref, cre_ref, cim_ref, dsk_ref,
                    mt_ref, bsre_ref, bsim_ref, ccat_ref, a16_ref):
    g = pl.program_id(0)
    are = are_ref[0]
    aim = aim_ref[0]
    dt = jnp.exp(ldt_ref[0])
    bre, bim = bre_ref[0], bim_ref[0]
    cre, cim = cre_ref[0], cim_ref[0]

    def lpow(e):
        mag = jnp.exp(are * dt * e)
        ang = aim * dt * e
        return mag * jnp.cos(ang), mag * jnp.sin(ang)

    one = jnp.ones((1, 1), F32)
    lb_re, lb_im = lpow(one)
    nr, ni = lb_re - 1.0, lb_im
    den = are * are + aim * aim
    coef_re = (nr * are + ni * aim) / den
    coef_im = (ni * are - nr * aim) / den
    bb_re = coef_re * bre - coef_im * bim
    bb_im = coef_re * bim + coef_im * bre

    p = lax.broadcasted_iota(jnp.int32, (S5_R, 1), 0).astype(F32)
    lane = lax.broadcasted_iota(jnp.int32, (1, LANES), 1)
    mine = (lane // S5_STATE) == (g % 2)
    low = lane < S5_STATE

    def outer(t_re, t_im, m_re, m_im):
        re = t_re[:, None, :] * m_re[None, :, :] - t_im[:, None, :] * m_im[None, :, :]
        im = t_re[:, None, :] * m_im[None, :, :] + t_im[:, None, :] * m_re[None, :, :]
        return re.reshape(S5_BLK, LANES), im.reshape(S5_BLK, LANES)

    half = float(S5_R // 2)
    e_re, e_im = lpow(p - half)
    f_re, f_im = lpow(half - p)
    p_re, p_im = outer(e_re, e_im, cre, cim)
    q_re, q_im = outer(f_re, f_im, bb_re, bb_im)
    q_re = jnp.where(low, q_re, 0.0)
    q_im = jnp.where(low, q_im, 0.0)
    mt = (lax.dot_general(p_re, q_re, _NT, precision=lax.Precision.HIGHEST, preferred_element_type=F32)
          - lax.dot_general(p_im, q_im, _NT, precision=lax.Precision.HIGHEST, preferred_element_type=F32))
    row = lax.broadcasted_iota(jnp.int32, (S5_BLK, S5_BLK), 0)
    col = lax.broadcasted_iota(jnp.int32, (S5_BLK, S5_BLK), 1)
    mt = jnp.where(row // S5_GROUP >= col // S5_GROUP, mt, 0.0)
    mt = mt + jnp.where(row == col, dsk_ref[0], 0.0)
    mt_ref[0] = mt.astype(mt_ref.dtype)

    g_re, g_im = lpow(float(S5_R - 1) - p)
    bs_re, bs_im = outer(g_re, g_im, bb_re, bb_im)
    bsre_ref[0] = jnp.where(mine, bs_re, 0.0).astype(bsre_ref.dtype)
    bsim_ref[0] = jnp.where(mine, bs_im, 0.0).astype(bsim_ref.dtype)

    h_re, h_im = lpow(p + 1.0)
    cs_re, cs_im = outer(h_re, h_im, cre, cim)
    ccat_ref[0, :, 0:LANES] = jnp.where(mine, cs_re, 0.0).astype(ccat_ref.dtype)
    ccat_ref[0, :, LANES:2 * LANES] = jnp.where(mine, -cs_im, 0.0).astype(ccat_ref.dtype)

    a_re, a_im = lpow(one * float(S5_R))
    rows8 = lax.broadcasted_iota(jnp.int32, (8, LANES), 0)
    a16_ref[0] = jnp.where(mine, jnp.where(rows8 == 0, a_re, jnp.where(rows8 == 1, a_im, 0.0)), 0.0)


def _s5_prep(are2, aim2, ldt, bre2, bim2, cre2, cim2, dsk_t):
    G = are2.shape[0]
    spec3 = lambda shape: pl.BlockSpec((1,) + shape, lambda g: (g, 0, 0))
    return pl.pallas_call(
        _s5_prep_kernel,
        grid=(G,),
        in_specs=[spec3((1, LANES)), spec3((1, LANES)), spec3((1, 1)),
                  spec3((S5_GROUP, LANES)), spec3((S5_GROUP, LANES)),
                  spec3((S5_GROUP, LANES)), spec3((S5_GROUP, LANES)), spec3((S5_BLK, 1))],
        out_specs=[spec3((S5_BLK, S5_BLK)), spec3((S5_BLK, LANES)), spec3((S5_BLK, LANES)),
                   spec3((S5_BLK, 2 * LANES)), spec3((8, LANES))],
        out_shape=[jax.ShapeDtypeStruct((G, S5_BLK, S5_BLK), BF16),
                   jax.ShapeDtypeStruct((G, S5_BLK, LANES), BF16),
                   jax.ShapeDtypeStruct((G, S5_BLK, LANES), BF16),
                   jax.ShapeDtypeStruct((G, S5_BLK, 2 * LANES), BF16),
                   jax.ShapeDtypeStruct((G, 8, LANES), F32)],
        compiler_params=_cparams(("parallel",)),
        name="s5_prep",
    )(are2, aim2, ldt, bre2, bim2, cre2, cim2, dsk_t)


def _s5_core_kernel(at_ref, mt_ref, bsre_ref, bsim_ref, ccat_ref, a16_ref, yt_ref,
                    zre_ref, zim_ref, xre_ref, xim_ref, *, nb, nj):
    for b in range(nb):
        rows = pl.ds(b * nj, nj)
        zre_ref[rows, :] = (lax.dot_general(at_ref[b, 0], bsre_ref[0], _TN, preferred_element_type=F32)
                            + lax.dot_general(at_ref[b, 1], bsre_ref[1], _TN, preferred_element_type=F32))
        zim_ref[rows, :] = (lax.dot_general(at_ref[b, 0], bsim_ref[0], _TN, preferred_element_type=F32)
                            + lax.dot_general(at_ref[b, 1], bsim_ref[1], _TN, preferred_element_type=F32))
    a_re = jnp.broadcast_to(a16_ref[0, 0:1, :] + a16_ref[1, 0:1, :], (nb, LANES))
    a_im = jnp.broadcast_to(a16_ref[0, 1:2, :] + a16_ref[1, 1:2, :], (nb, LANES))

    def step(j, carry):
        xr, xi = carry
        idx = pl.ds(j, nb, stride=nj)
        xre_ref[idx, :] = xr
        xim_ref[idx, :] = xi
        zr = zre_ref[idx, :]
        zi = zim_ref[idx, :]
        return a_re * xr - a_im * xi + zr, a_re * xi + a_im * xr + zi

    zero = jnp.zeros((nb, LANES), F32)
    lax.fori_loop(0, nj, step, (zero, zero))
    for b in range(nb):
        rows = pl.ds(b * nj, nj)
        xcat = jnp.concatenate([xre_ref[rows, :], xim_ref[rows, :]], axis=1).astype(BF16)
        for i in range(2):
            yt = (jnp.dot(mt_ref[i], at_ref[b, i], preferred_element_type=F32)
                  + lax.dot_general(ccat_ref[i], xcat, _NT, preferred_element_type=F32))
            yt_ref[b, i] = yt.astype(yt_ref.dtype)


def _s5_core(at, mt, bsre, bsim, ccat, a16):
    nb, G, _, nj = at.shape
    pair = lambda shape: pl.BlockSpec((2,) + shape, lambda p: (p, 0, 0))
    data = pl.BlockSpec((nb, 2, S5_BLK, nj), lambda p: (0, p, 0, 0))
    return pl.pallas_call(
        functools.partial(_s5_core_kernel, nb=nb, nj=nj),
        grid=(G // 2,),
        in_specs=[data, pair((S5_BLK, S5_BLK)), pair((S5_BLK, LANES)),
                  pair((S5_BLK, LANES)), pair((S5_BLK, 2 * LANES)), pair((8, LANES))],
        out_specs=data,
        out_shape=jax.ShapeDtypeStruct(at.shape, BF16),
        scratch_shapes=[pltpu.VMEM((nb * nj, LANES), F32)] * 4,
        compiler_params=_cparams(("parallel",)),
        name="s5_core",
    )(at, mt, bsre, bsim, ccat, a16)


def _ret_kernel(q_ref, k_ref, v_ref, cos_ref, sin_ref, intra_ref, qdec_ref, kdec_ref, cdec_ref,
                o_ref, s_ref, qs_ref, ks_ref, *, n_chunks):
    @pl.when(pl.program_id(1) == 0)
    def _():
        s_ref[...] = jnp.zeros(s_ref.shape, s_ref.dtype)

    cos = cos_ref[...]
    sin = sin_ref[...]
    width = q_ref.shape[-1]
    lane = lax.broadcasted_iota(jnp.int32, cos.shape, 1)
    first = (lane % RET_DK) < (RET_DK // 2)

    def rope(x):
        partner = jnp.where(first, pltpu.roll(x, width - RET_DK // 2, 1), pltpu.roll(x, RET_DK // 2, 1))
        return x * cos + partner * sin

    qs_ref[...] = (rope(q_ref[...].astype(F32)) * (RET_DK ** -0.5)).astype(BF16)
    ks_ref[...] = rope(k_ref[...].astype(F32)).astype(BF16)

    lane_p = lax.broadcasted_iota(jnp.int32, (CHUNK, LANES), 1)
    for c in range(n_chunks):
        rows = pl.ds(c * CHUNK, CHUNK)
        for h in range(RET_HEADS):
            pr = pl.ds((h // 2) * LANES, LANES)
            qp = qs_ref[rows, pr]
            kp = ks_ref[rows, pr]
            km = jnp.where((lane_p // RET_DK) == (h % 2), kp, jnp.zeros_like(kp))
            vh = v_ref[rows, pl.ds(h * RET_DV, RET_DV)]
            sc = lax.dot_general(qp, km, _NT, preferred_element_type=F32) * intra_ref[h]
            o = jnp.dot(sc.astype(BF16), vh, preferred_element_type=F32)
            s_prev = s_ref[h]
            o = o + qdec_ref[h] * jnp.dot(qp, s_prev.astype(BF16), preferred_element_type=F32)
            vd = (vh.astype(F32) * kdec_ref[h]).astype(BF16)
            kv = lax.dot_general(km, vd, _TN, preferred_element_type=F32)
            s_ref[h] = cdec_ref[h] * s_prev + kv
            mu = jnp.mean(o, axis=-1, keepdims=True)
            var = jnp.mean(jnp.square(o - mu), axis=-1, keepdims=True)
            o_ref[rows, pl.ds(h * RET_DV, RET_DV)] = ((o - mu) * lax.rsqrt(var + EPS)).astype(o_ref.dtype)


def _retention(q, k, v, cos, sin, intra, qdec, kdec, cdec, nb, seq, tb):
    rows = q.shape[0]
    nt = seq // tb
    row_blk = lambda wd: pl.BlockSpec((tb, wd), lambda b, t: (b * nt + t, 0))
    pos_blk = lambda wd: pl.BlockSpec((tb, wd), lambda b, t: (t, 0))
    full = lambda a: pl.BlockSpec(a.shape, lambda b, t: (0,) * a.ndim)
    qk_w, v_w = q.shape[-1], v.shape[-1]
    return pl.pallas_call(
        functools.partial(_ret_kernel, n_chunks=tb // CHUNK),
        grid=(nb, nt),
        in_specs=[row_blk(qk_w), row_blk(qk_w), row_blk(v_w), pos_blk(qk_w), pos_blk(qk_w),
                  full(intra), full(qdec), full(kdec), full(cdec)],
        out_specs=row_blk(v_w),
        out_shape=jax.ShapeDtypeStruct((rows, v_w), BF16),
        scratch_shapes=[pltpu.VMEM((RET_HEADS, LANES, RET_DV), F32),
                        pltpu.VMEM((tb, qk_w), BF16), pltpu.VMEM((tb, qk_w), BF16)],
        compiler_params=_cparams(("parallel", "arbitrary")),
        name="retention",
    )(q, k, v, cos, sin, intra, qdec, kdec, cdec)


def _merge0_kernel(x_ref, y5t_ref, g5t_ref, yr_ref, gr_ref, gwt_ref, gb_ref, wo5_ref, wor_ref, o_ref):
    nj = x_ref.shape[1]
    y = jax.nn.gelu(y5t_ref[0].astype(F32).reshape(-1, nj))
    gate = jnp.dot(gwt_ref[...], y.astype(BF16), preferred_element_type=F32) + gb_ref[...]
    a = y * jax.nn.sigmoid(gate) * jax.nn.silu(g5t_ref[0].astype(F32))
    r = yr_ref[0].astype(F32) * jax.nn.silu(gr_ref[0].astype(F32))
    o_ref[0] = (x_ref[0]
                + lax.dot_general(a.astype(BF16), wo5_ref[...], _TN, preferred_element_type=F32)
                + jnp.dot(r.astype(BF16), wor_ref[...], preferred_element_type=F32))


def _merge0(x3, y5t, g5t, yr3, gr3, gwt, gb_col, wo5, wor, d):
    nb, nj, _ = x3.shape
    G = y5t.shape[1]
    s5_w = g5t.shape[2]
    ret_w = wor.shape[0]
    y5t5 = y5t.reshape(nb, G, S5_R, S5_GROUP, nj)
    strided = lambda wd: pl.BlockSpec((1, nj, wd), lambda b, s: (b, 0, s))
    full = lambda a: pl.BlockSpec(a.shape, lambda b, s: (0,) * a.ndim)
    return pl.pallas_call(
        _merge0_kernel,
        grid=(nb, S5_R),
        in_specs=[strided(d),
                  pl.BlockSpec((1, G, None, S5_GROUP, nj), lambda b, s: (b, 0, s, 0, 0)),
                  pl.BlockSpec((1, None, s5_w, nj), lambda b, s: (b, s, 0, 0)),
                  strided(ret_w), strided(ret_w), full(gwt), full(gb_col), full(wo5), full(wor)],
        out_specs=strided(d),
        out_shape=jax.ShapeDtypeStruct(x3.shape, F32),
        compiler_params=_cparams(("parallel", "parallel")),
        name="merge0",
    )(x3, y5t5, g5t, yr3, gr3, gwt, gb_col, wo5, wor)


def _layer1_kernel(x_ref, g_ref, win_ref, cw_ref, cb_ref, lg_ref, lb_ref, wout_ref, fg_ref, o_ref,
                   ubuf_ref, *, tm, width):
    x = x_ref[...]
    hn = _rms(x, g_ref[...]).astype(BF16)
    a = jnp.dot(hn, win_ref[:, 0:width], preferred_element_type=F32)
    b = jnp.dot(hn, win_ref[:, width:2 * width], preferred_element_type=F32)
    gate = jnp.dot(hn, win_ref[:, 2 * width:3 * width], preferred_element_type=F32)

    @pl.when(pl.program_id(1) == 0)
    def _():
        ubuf_ref[0:CONV_HALO, :] = jnp.zeros((CONV_HALO, width), F32)

    ubuf_ref[CONV_HALO:CONV_HALO + tm, :] = a * jax.nn.sigmoid(b)
    base = CONV_HALO - (CONV_K - 1)
    acc = jnp.zeros((tm, width), F32) + cb_ref[...]
    for kk in range(CONV_K):
        acc = acc + ubuf_ref[base + kk:base + kk + tm, :] * cw_ref[kk:kk + 1, :]
    ubuf_ref[0:CONV_HALO, :] = ubuf_ref[tm:tm + CONV_HALO, :]

    mu = jnp.mean(acc, axis=-1, keepdims=True)
    var = jnp.mean(jnp.square(acc - mu), axis=-1, keepdims=True)
    uf = (acc - mu) * lax.rsqrt(var + EPS) * lg_ref[...] + lb_ref[...]
    u = jax.nn.silu(uf) * jax.nn.silu(gate)
    x1 = x + jnp.dot(u.astype(BF16), wout_ref[...], preferred_element_type=F32)
    o_ref[...] = _rms(x1, fg_ref[...])


def _layer1(x2, g, win, cw, cb, lg, lb, wout, fg, nb, seq, tm):
    rows, d = x2.shape
    width = wout.shape[0]
    nt = seq // tm
    row_blk = pl.BlockSpec((tm, d), lambda b, t: (b * nt + t, 0))
    full = lambda a: pl.BlockSpec(a.shape, lambda b, t: (0,) * a.ndim)
    return pl.pallas_call(
        functools.partial(_layer1_kernel, tm=tm, width=width),
        grid=(nb, nt),
        in_specs=[row_blk, full(g), full(win), full(cw), full(cb), full(lg), full(lb), full(wout), full(fg)],
        out_specs=row_blk,
        out_shape=jax.ShapeDtypeStruct((rows, d), F32),
        scratch_shapes=[pltpu.VMEM((CONV_HALO + tm, width), F32)],
        compiler_params=_cparams(("parallel", "arbitrary")),
        name="layer1",
    )(x2, g, win, cw, cb, lg, lb, wout, fg)


def _pick_tile(n, pref):
    t = min(n, pref)
    while n % t:
        t //= 2
    return t


def kernel(x, norm_g, final_g, w_in_ab, s5_a_re, s5_a_im, s5_log_dt, s5_b_re, s5_b_im, s5_c_re, s5_c_im,
           s5_d, s5_glu_w, s5_glu_b, w_out_ab, w_in_c, conv_w, conv_b, conv_ln_g, conv_ln_b, w_out_c):
    nb, seq, d = x.shape
    rows = nb * seq
    s5_w = s5_glu_w.shape[-1]
    G = s5_w // S5_GROUP
    ret_qk = RET_HEADS * RET_DK
    ret_w = RET_HEADS * RET_DV
    nj = seq // S5_R
    x2 = x.reshape(rows, d)

    w_in = w_in_ab[0].astype(BF16)
    widths = (ret_qk, ret_qk, ret_w, ret_w)
    q, k, v, gr = _proj0(x2, norm_g[0:1], w_in[:, 2 * s5_w:], widths, _pick_tile(rows, 512))
    x3 = x.reshape(nb, nj, S5_R * d)
    at, g5t = _proj_u(x3, norm_g[0:1], w_in[:, :2 * s5_w].T, s5_w, 4)

    dup = lambda a: jnp.concatenate([a, a], axis=-1)
    are2 = dup(s5_a_re[0])[:, None, :]
    aim2 = dup(s5_a_im[0])[:, None, :]
    ldt = s5_log_dt[0][:, None, None]
    bre2 = dup(jnp.swapaxes(s5_b_re[0], 1, 2))
    bim2 = dup(jnp.swapaxes(s5_b_im[0], 1, 2))
    cre2 = dup(s5_c_re[0])
    cim2 = dup(s5_c_im[0])
    dsk_t = jnp.tile(s5_d[0].reshape(G, 1, S5_GROUP), (1, S5_R, 1)).reshape(G, S5_BLK, 1)
    mt, bsre, bsim, ccat, a16 = _s5_prep(are2, aim2, ldt, bre2, bim2, cre2, cim2, dsk_t)

    y5t = _s5_core(at, mt, bsre, bsim, ccat, a16)

    pos = jnp.arange(seq, dtype=F32)[:, None]
    half = RET_DK // 2
    freqs = ROPE_BASE ** (-jnp.arange(half, dtype=F32) / half)
    ang = pos * freqs[None, :]
    cos_h = jnp.concatenate([jnp.cos(ang), jnp.cos(ang)], axis=-1)
    sin_h = jnp.concatenate([-jnp.sin(ang), jnp.sin(ang)], axis=-1)
    cos_t = jnp.tile(cos_h, (1, RET_HEADS))
    sin_t = jnp.tile(sin_h, (1, RET_HEADS))
    log_g = jnp.log(1.0 - 2.0 ** (-5.0 - jnp.arange(RET_HEADS, dtype=F32)))
    idx = jnp.arange(CHUNK, dtype=F32)
    intra = jnp.exp(log_g[:, None, None] * jnp.abs(idx[:, None] - idx[None, :]))
    qdec = jnp.exp(log_g[:, None] * (idx + 1.0)[None, :])[:, :, None]
    kdec = jnp.exp(log_g[:, None] * (CHUNK - 1 - idx)[None, :])[:, :, None]
    cdec = jnp.broadcast_to(jnp.exp(log_g * CHUNK)[:, None, None], (RET_HEADS, 1, RET_DV))
    yr = _retention(q, k, v, cos_t, sin_t, intra, qdec, kdec, cdec, nb, seq, _pick_tile(seq, 512))

    wo = w_out_ab[0].astype(BF16)
    x3 = _merge0(x3, y5t, g5t, yr.reshape(nb, nj, S5_R * ret_w), gr.reshape(nb, nj, S5_R * ret_w),
                 s5_glu_w[0].T.astype(BF16), s5_glu_b[0][:, None], wo[:s5_w], wo[s5_w:], d)
    x2 = x3.reshape(rows, d)

    out = _layer1(x2, norm_g[1:2], w_in_c[0].astype(BF16), conv_w[0], conv_b[0:1], conv_ln_g[0:1],
                  conv_ln_b[0:1], w_out_c[0].astype(BF16), final_g[None, :], nb, seq, _pick_tile(seq, 512))
    return out.reshape(nb, seq, d)
```

```python
import functools
import math

import jax
import jax.numpy as jnp
from jax import lax
from jax.experimental import pallas as pl
from jax.experimental.pallas import tpu as pltpu

F32 = jnp.float32
BF16 = jnp.bfloat16

EPS = 1e-6
CHUNK = 64
S5_GROUP = 16
S5_STATE = 64
S5_R = 16
S5_BLK = S5_R * S5_GROUP
RET_HEADS = 8
RET_DK = 64
RET_DV = 128
ROPE_BASE = 10000.0
CONV_K = 31
CONV_HALO = 32

LANES = 128
VMEM_LIMIT_BYTES = 56 * 1024 * 1024

_NT = (((1,), (1,)), ((), ()))
_TN = (((0,), (0,)), ((), ()))


def _cparams(sem):
    return pltpu.CompilerParams(dimension_semantics=sem, vmem_limit_bytes=VMEM_LIMIT_BYTES)


def _rms(x, g):
    return x * lax.rsqrt(jnp.mean(x * x, axis=-1, keepdims=True) + EPS) * g


def _to_frames(val, scr_ref, dst_ref):
    nc, rows, _ = scr_ref.shape
    for c in range(nc):
        scr_ref[c] = val[:, c * LANES:(c + 1) * LANES]
    for s in range(S5_R):
        dst_ref[0, s] = jnp.concatenate(
            [scr_ref[c, pl.ds(s, rows // S5_R, stride=S5_R), :] for c in range(nc)], axis=1).astype(dst_ref.dtype)


def _from_frames(val, scr_ref, dst_ref):
    nc, rows, _ = scr_ref.shape
    jt = rows // S5_R
    for s in range(S5_R):
        for c in range(nc):
            scr_ref[c, pl.ds(s, jt, stride=S5_R), :] = val[s * jt:(s + 1) * jt, c * LANES:(c + 1) * LANES]
    dst_ref[...] = jnp.concatenate([scr_ref[c] for c in range(nc)], axis=1)


def _proj0_kernel(x_ref, g_ref, w_ref, q_ref, k_ref, v_ref, gr3_ref, hn3_ref, x3_ref, scr_ref):
    x = x_ref[...]
    hn32 = _rms(x, g_ref[...])
    hn = hn32.astype(BF16)
    off = 0
    for ref in (q_ref, k_ref, v_ref):
        wd = ref.shape[-1]
        ref[...] = jnp.dot(hn, w_ref[:, off:off + wd], preferred_element_type=F32).astype(ref.dtype)
        off += wd
    _to_frames(hn32, scr_ref, hn3_ref)
    _to_frames(x, scr_ref, x3_ref)
    _to_frames(jnp.dot(hn, w_ref[:, off:], preferred_element_type=F32), scr_ref, gr3_ref)


def _proj0(x2, g, w, nb, seq, qk_w, v_w, tm):
    rows, d = x2.shape
    nt = seq // tm
    jt = tm // S5_R
    nj = seq // S5_R
    nat = lambda wd: pl.BlockSpec((tm, wd), lambda b, t: (b * nt + t, 0))
    fm = lambda wd: pl.BlockSpec((1, S5_R, jt, wd), lambda b, t: (b, 0, t, 0))
    fm_shape = lambda wd, dt: jax.ShapeDtypeStruct((nb, S5_R, nj, wd), dt)
    assert d == v_w, "the strided-read scratch is shared between hn and the retention gate"
    return pl.pallas_call(
        _proj0_kernel,
        grid=(nb, nt),
        in_specs=[nat(d), pl.BlockSpec((1, d), lambda b, t: (0, 0)), pl.BlockSpec(w.shape, lambda b, t: (0, 0))],
        out_specs=[nat(qk_w), nat(qk_w), nat(v_w), fm(v_w), fm(d), fm(d)],
        out_shape=[jax.ShapeDtypeStruct((rows, qk_w), BF16), jax.ShapeDtypeStruct((rows, qk_w), BF16),
                   jax.ShapeDtypeStruct((rows, v_w), BF16), fm_shape(v_w, BF16), fm_shape(d, BF16),
                   fm_shape(d, F32)],
        scratch_shapes=[pltpu.VMEM((d // LANES, tm, LANES), F32)],
        compiler_params=_cparams(("parallel", "parallel")),
        name="proj0",
    )(x2, g, w)


def _proj_u_kernel(hn3_ref, wt_ref, at_ref, g5t_ref, *, sg, s5_w):
    nj = hn3_ref.shape[2]
    for sl in range(sg):
        ut = lax.dot_general(wt_ref[...], hn3_ref[0, sl], _NT, preferred_element_type=F32)
        at_ref[0, :, sl * S5_GROUP:(sl + 1) * S5_GROUP, :] = (
            ut[:s5_w].reshape(s5_w // S5_GROUP, S5_GROUP, nj).astype(at_ref.dtype))
        g5t_ref[0, sl] = ut[s5_w:].astype(g5t_ref.dtype)


def _proj_u(hn3, wt, s5_w, sg):
    nb, _, nj, d = hn3.shape
    G = s5_w // S5_GROUP
    return pl.pallas_call(
        functools.partial(_proj_u_kernel, sg=sg, s5_w=s5_w),
        grid=(nb, S5_R // sg),
        in_specs=[
            pl.BlockSpec((1, sg, nj, d), lambda b, s: (b, s, 0, 0)),
            pl.BlockSpec(wt.shape, lambda b, s: (0, 0)),
        ],
        out_specs=[
            pl.BlockSpec((1, G, sg * S5_GROUP, nj), lambda b, s: (b, 0, s, 0)),
            pl.BlockSpec((1, sg, s5_w, nj), lambda b, s: (b, s, 0, 0)),
        ],
        out_shape=[jax.ShapeDtypeStruct((nb, G, S5_BLK, nj), BF16),
                   jax.ShapeDtypeStruct((nb, S5_R, s5_w, nj), BF16)],
        compiler_params=_cparams(("parallel", "parallel")),
        name="proj_u",
    )(hn3, wt)


def _s5_prep_kernel(are_ref, aim_ref, ldt_ref, bre_ref, bim_ref, cre_ref, cim_ref, dsk_ref,
                    mt_ref, bsre_ref, bsim_ref, ccat_ref, a16_ref):
    g = pl.program_id(0)
    are = are_ref[0]
    aim = aim_ref[0]
    dt = jnp.exp(ldt_ref[0])
    bre, bim = bre_ref[0], bim_ref[0]
    cre, cim = cre_ref[0], cim_ref[0]

    def lpow(e):
        mag = jnp.exp(are * dt * e)
        ang = aim * dt * e
        return mag * jnp.cos(ang), mag * jnp.sin(ang)

    one = jnp.ones((1, 1), F32)
    lb_re, lb_im = lpow(one)
    nr, ni = lb_re - 1.0, lb_im
    den = are * are + aim * aim
    coef_re = (nr * are + ni * aim) / den
    coef_im = (ni * are - nr * aim) / den
    bb_re = coef_re * bre - coef_im * bim
    bb_im = coef_re * bim + coef_im * bre

    p = lax.broadcasted_iota(jnp.int32, (S5_R, 1), 0).astype(F32)
    lane = lax.broadcasted_iota(jnp.int32, (1, LANES), 1)
    mine = (lane // S5_STATE) == (g % 2)
    low = lane < S5_STATE

    def outer(t_re, t_im, m_re, m_im):
        re = t_re[:, None, :] * m_re[None, :, :] - t_im[:, None, :] * m_im[None, :, :]
        im = t_re[:, None, :] * m_im[None, :, :] + t_im[:, None, :] * m_re[None, :, :]
        return re.reshape(S5_BLK, LANES), im.reshape(S5_BLK, LANES)

    half = float(S5_R // 2)
    e_re, e_im = lpow(p - half)
    f_re, f_im = lpow(half - p)
    p_re, p_im = outer(e_re, e_im, cre, cim)
    q_re, q_im = outer(f_re, f_im, bb_re, bb_im)
    q_re = jnp.where(low, q_re, 0.0)
    q_im = jnp.where(low, q_im, 0.0)
    mt = (lax.dot_general(p_re, q_re, _NT, precision=lax.Precision.HIGHEST, preferred_element_type=F32)
          - lax.dot_general(p_im, q_im, _NT, precision=lax.Precision.HIGHEST, preferred_element_type=F32))
    row = lax.broadcasted_iota(jnp.int32, (S5_BLK, S5_BLK), 0)
    col = lax.broadcasted_iota(jnp.int32, (S5_BLK, S5_BLK), 1)
    mt = jnp.where(row // S5_GROUP >= col // S5_GROUP, mt, 0.0)
    mt = mt + jnp.where(row == col, dsk_ref[0], 0.0)
    mt_ref[0] = mt.astype(mt_ref.dtype)

    g_re, g_im = lpow(float(S5_R - 1) - p)
    bs_re, bs_im = outer(g_re, g_im, bb_re, bb_im)
    bsre_ref[0] = jnp.where(mine, bs_re, 0.0).astype(bsre_ref.dtype)
    bsim_ref[0] = jnp.where(mine, bs_im, 0.0).astype(bsim_ref.dtype)

    h_re, h_im = lpow(p + 1.0)
    cs_re, cs_im = outer(h_re, h_im, cre, cim)
    ccat_ref[0, :, 0:LANES] = jnp.where(mine, cs_re, 0.0).astype(ccat_ref.dtype)
    ccat_ref[0, :, LANES:2 * LANES] = jnp.where(mine, -cs_im, 0.0).astype(ccat_ref.dtype)

    a_re, a_im = lpow(one * float(S5_R))
    rows8 = lax.broadcasted_iota(jnp.int32, (8, LANES), 0)
    a16_ref[0] = jnp.where(mine, jnp.where(rows8 == 0, a_re, jnp.where(rows8 == 1, a_im, 0.0)), 0.0)


def _s5_prep(are2, aim2, ldt, bre2, bim2, cre2, cim2, dsk_t):
    G = are2.shape[0]
    spec3 = lambda shape: pl.BlockSpec((1,) + shape, lambda g: (g, 0, 0))
    return pl.pallas_call(
        _s5_prep_kernel,
        grid=(G,),
        in_specs=[spec3((1, LANES)), spec3((1, LANES)), spec3((1, 1)),
                  spec3((S5_GROUP, LANES)), spec3((S5_GROUP, LANES)),
                  spec3((S5_GROUP, LANES)), spec3((S5_GROUP, LANES)), spec3((S5_BLK, 1))],
        out_specs=[spec3((S5_BLK, S5_BLK)), spec3((S5_BLK, LANES)), spec3((S5_BLK, LANES)),
                   spec3((S5_BLK, 2 * LANES)), spec3((8, LANES))],
        out_shape=[jax.ShapeDtypeStruct((G, S5_BLK, S5_BLK), BF16),
                   jax.ShapeDtypeStruct((G, S5_BLK, LANES), BF16),
                   jax.ShapeDtypeStruct((G, S5_BLK, LANES), BF16),
                   jax.ShapeDtypeStruct((G, S5_BLK, 2 * LANES), BF16),
                   jax.ShapeDtypeStruct((G, 8, LANES), F32)],
        compiler_params=_cparams(("parallel",)),
        name="s5_prep",
    )(are2, aim2, ldt, bre2, bim2, cre2, cim2, dsk_t)


def _s5_core_kernel(at_ref, mt_ref, bsre_ref, bsim_ref, ccat_ref, a16_ref, yt_ref,
                    zre_ref, zim_ref, xre_ref, xim_ref, *, nb, nj):
    for b in range(nb):
        rows = pl.ds(b * nj, nj)
        zre_ref[rows, :] = (lax.dot_general(at_ref[b, 0], bsre_ref[0], _TN, preferred_element_type=F32)
                            + lax.dot_general(at_ref[b, 1], bsre_ref[1], _TN, preferred_element_type=F32))
        zim_ref[rows, :] = (lax.dot_general(at_ref[b, 0], bsim_ref[0], _TN, preferred_element_type=F32)
                            + lax.dot_general(at_ref[b, 1], bsim_ref[1], _TN, preferred_element_type=F32))
    a_re = jnp.broadcast_to(a16_ref[0, 0:1, :] + a16_ref[1, 0:1, :], (nb, LANES))
    a_im = jnp.broadcast_to(a16_ref[0, 1:2, :] + a16_ref[1, 1:2, :], (nb, LANES))

    def step(j, carry):
        xr, xi = carry
        idx = pl.ds(j, nb, stride=nj)
        xre_ref[idx, :] = xr
        xim_ref[idx, :] = xi
        zr = zre_ref[idx, :]
        zi = zim_ref[idx, :]
        return a_re * xr - a_im * xi + zr, a_re * xi + a_im * xr + zi

    zero = jnp.zeros((nb, LANES), F32)
    lax.fori_loop(0, nj, step, (zero, zero))
    for b in range(nb):
        rows = pl.ds(b * nj, nj)
        xcat = jnp.concatenate([xre_ref[rows, :], xim_ref[rows, :]], axis=1).astype(BF16)
        for i in range(2):
            yt = (jnp.dot(mt_ref[i], at_ref[b, i], preferred_element_type=F32)
                  + lax.dot_general(ccat_ref[i], xcat, _NT, preferred_element_type=F32))
            yt_ref[b, i] = yt.astype(yt_ref.dtype)


def _s5_core(at, mt, bsre, bsim, ccat, a16):
    nb, G, _, nj = at.shape
    pair = lambda shape: pl.BlockSpec((2,) + shape, lambda p: (p, 0, 0))
    data = pl.BlockSpec((nb, 2, S5_BLK, nj), lambda p: (0, p, 0, 0))
    return pl.pallas_call(
        functools.partial(_s5_core_kernel, nb=nb, nj=nj),
        grid=(G // 2,),
        in_specs=[data, pair((S5_BLK, S5_BLK)), pair((S5_BLK, LANES)),
                  pair((S5_BLK, LANES)), pair((S5_BLK, 2 * LANES)), pair((8, LANES))],
        out_specs=data,
        out_shape=jax.ShapeDtypeStruct(at.shape, BF16),
        scratch_shapes=[pltpu.VMEM((nb * nj, LANES), F32)] * 4,
        compiler_params=_cparams(("parallel",)),
        name="s5_core",
    )(at, mt, bsre, bsim, ccat, a16)


def _ret_kernel(q_ref, k_ref, v_ref, cos_ref, sin_ref, intra_ref, qdec_ref, kdec_ref, cdec_ref,
                o3_ref, s_ref, qs_ref, ks_ref, o_ref, *, n_chunks):
    @pl.when(pl.program_id(1) == 0)
    def _():
        s_ref[...] = jnp.zeros(s_ref.shape, s_ref.dtype)

    cos = cos_ref[...]
    sin = sin_ref[...]
    width = q_ref.shape[-1]
    lane = lax.broadcasted_iota(jnp.int32, cos.shape, 1)
    first = (lane % RET_DK) < (RET_DK // 2)

    def rope(x):
        partner = jnp.where(first, pltpu.roll(x, width - RET_DK // 2, 1), pltpu.roll(x, RET_DK // 2, 1))
        return x * cos + partner * sin

    qs_ref[...] = (rope(q_ref[...].astype(F32)) * (RET_DK ** -0.5)).astype(BF16)
    ks_ref[...] = rope(k_ref[...].astype(F32)).astype(BF16)

    lane_p = lax.broadcasted_iota(jnp.int32, (CHUNK, LANES), 1)
    for c in range(n_chunks):
        rows = pl.ds(c * CHUNK, CHUNK)
        for h in range(RET_HEADS):
            pr = pl.ds((h // 2) * LANES, LANES)
            qp = qs_ref[rows, pr]
            kp = ks_ref[rows, pr]
            km = jnp.where((lane_p // RET_DK) == (h % 2), kp, jnp.zeros_like(kp))
            vh = v_ref[rows, pl.ds(h * RET_DV, RET_DV)]
            sc = lax.dot_general(qp, km, _NT, preferred_element_type=F32) * intra_ref[h]
            o = jnp.dot(sc.astype(BF16), vh, preferred_element_type=F32)
            s_prev = s_ref[h]
            o = o + qdec_ref[h] * jnp.dot(qp, s_prev.astype(BF16), preferred_element_type=F32)
            vd = (vh.astype(F32) * kdec_ref[h]).astype(BF16)
            kv = lax.dot_general(km, vd, _TN, preferred_element_type=F32)
            s_ref[h] = cdec_ref[h] * s_prev + kv
            mu = jnp.mean(o, axis=-1, keepdims=True)
            var = jnp.mean(jnp.square(o - mu), axis=-1, keepdims=True)
            o_ref[h, rows, :] = (o - mu) * lax.rsqrt(var + EPS)
    jt = o3_ref.shape[2]
    for s in range(S5_R):
        o3_ref[0, s] = jnp.concatenate(
            [o_ref[h, pl.ds(s, jt, stride=S5_R), :] for h in range(RET_HEADS)], axis=1).astype(o3_ref.dtype)


def _retention(q, k, v, cos, sin, intra, qdec, kdec, cdec, nb, seq, tb):
    nt = seq // tb
    row_blk = lambda wd: pl.BlockSpec((tb, wd), lambda b, t: (b * nt + t, 0))
    pos_blk = lambda wd: pl.BlockSpec((tb, wd), lambda b, t: (t, 0))
    full = lambda a: pl.BlockSpec(a.shape, lambda b, t: (0,) * a.ndim)
    qk_w, v_w = q.shape[-1], v.shape[-1]
    return pl.pallas_call(
        functools.partial(_ret_kernel, n_chunks=tb // CHUNK),
        grid=(nb, nt),
        in_specs=[row_blk(qk_w), row_blk(qk_w), row_blk(v_w), pos_blk(qk_w), pos_blk(qk_w),
                  full(intra), full(qdec), full(kdec), full(cdec)],
        out_specs=pl.BlockSpec((1, S5_R, tb // S5_R, v_w), lambda b, t: (b, 0, t, 0)),
        out_shape=jax.ShapeDtypeStruct((nb, S5_R, seq // S5_R, v_w), BF16),
        scratch_shapes=[pltpu.VMEM((RET_HEADS, LANES, RET_DV), F32),
                        pltpu.VMEM((tb, qk_w), BF16), pltpu.VMEM((tb, qk_w), BF16),
                        pltpu.VMEM((RET_HEADS, tb, RET_DV), F32)],
        compiler_params=_cparams(("parallel", "arbitrary")),
        name="retention",
    )(q, k, v, cos, sin, intra, qdec, kdec, cdec)


def _merge0_kernel(x_ref, y5t_ref, g5t_ref, yr_ref, gr_ref, gwt_ref, gb_ref, wo5_ref, wor_ref, o_ref):
    nj = x_ref.shape[1]
    y = jax.nn.gelu(y5t_ref[0].astype(F32).reshape(-1, nj))
    gate = jnp.dot(gwt_ref[...], y.astype(BF16), preferred_element_type=F32) + gb_ref[...]
    a = y * jax.nn.sigmoid(gate) * jax.nn.silu(g5t_ref[0].astype(F32))
    r = yr_ref[0].astype(F32) * jax.nn.silu(gr_ref[0].astype(F32))
    o_ref[0] = (x_ref[0]
                + lax.dot_general(a.astype(BF16), wo5_ref[...], _TN, preferred_element_type=F32)
                + jnp.dot(r.astype(BF16), wor_ref[...], preferred_element_type=F32))


def _merge0(x3, y5t, g5t, yr3, gr3, gwt, gb_col, wo5, wor):
    nb, _, nj, d = x3.shape
    G = y5t.shape[1]
    s5_w = g5t.shape[2]
    ret_w = wor.shape[0]
    y5t5 = y5t.reshape(nb, G, S5_R, S5_GROUP, nj)
    strided = lambda wd: pl.BlockSpec((1, None, nj, wd), lambda b, s: (b, s, 0, 0))
    full = lambda a: pl.BlockSpec(a.shape, lambda b, s: (0,) * a.ndim)
    return pl.pallas_call(
        _merge0_kernel,
        grid=(nb, S5_R),
        in_specs=[strided(d),
                  pl.BlockSpec((1, G, None, S5_GROUP, nj), lambda b, s: (b, 0, s, 0, 0)),
                  pl.BlockSpec((1, None, s5_w, nj), lambda b, s: (b, s, 0, 0)),
                  strided(ret_w), strided(ret_w), full(gwt), full(gb_col), full(wo5), full(wor)],
        out_specs=strided(d),
        out_shape=jax.ShapeDtypeStruct(x3.shape, F32),
        compiler_params=_cparams(("parallel", "parallel")),
        name="merge0",
    )(x3, y5t5, g5t, yr3, gr3, gwt, gb_col, wo5, wor)


CONV_CARRY = 8
CONV_TB = 8
assert (CONV_K - 1 + S5_R - 1) // S5_R <= CONV_CARRY


def _layer1_kernel(x_ref, g_ref, win_ref, cw_ref, cb_ref, lg_ref, lb_ref, wout_ref, fg_ref, o_ref,
                   ubuf_ref, sh_ref, acc_ref, scr_ref, *, jt, width):
    rows = S5_R * jt
    x = x_ref[0].reshape(rows, x_ref.shape[-1])
    hn = _rms(x, g_ref[...]).astype(BF16)
    a = jnp.dot(hn, win_ref[:, 0:width], preferred_element_type=F32)
    b = jnp.dot(hn, win_ref[:, width:2 * width], preferred_element_type=F32)
    gate = jnp.dot(hn, win_ref[:, 2 * width:3 * width], preferred_element_type=F32)

    @pl.when(pl.program_id(1) == 0)
    def _():
        ubuf_ref[:, 0:CONV_CARRY, :] = jnp.zeros((S5_R, CONV_CARRY, width), F32)

    ubuf_ref[:, CONV_CARRY:CONV_CARRY + jt, :] = (a * jax.nn.sigmoid(b)).reshape(S5_R, jt, width)
    n_shift = (CONV_K - 1 + S5_R - 1) // S5_R
    for dl in range(1, n_shift + 1):
        sh_ref[dl - 1] = ubuf_ref[:, CONV_CARRY - dl:CONV_CARRY - dl + jt, :]
    for c in range(width // LANES):
        cols = pl.ds(c * LANES, LANES)
        for s0 in range(0, S5_R, CONV_TB):
            targets = range(s0, s0 + CONV_TB)
            accs = {s: jnp.broadcast_to(cb_ref[:, cols], (jt, LANES)) for s in targets}
            for e in range(s0 - (CONV_K - 1), s0 + CONV_TB):
                plane, dl = e % S5_R, -(e // S5_R)
                src = (ubuf_ref[plane, CONV_CARRY:CONV_CARRY + jt, cols] if dl == 0
                       else sh_ref[dl - 1, plane, :, cols])
                for s in targets:
                    kk = e + (CONV_K - 1) - s
                    if 0 <= kk < CONV_K:
                        accs[s] = accs[s] + src * cw_ref[kk:kk + 1, cols]
            for s in targets:
                acc_ref[s * jt:(s + 1) * jt, cols] = accs[s]
    ubuf_ref[:, 0:CONV_CARRY, :] = ubuf_ref[:, jt:jt + CONV_CARRY, :]

    acc = acc_ref[...]
    mu = jnp.mean(acc, axis=-1, keepdims=True)
    var = jnp.mean(jnp.square(acc - mu), axis=-1, keepdims=True)
    uf = (acc - mu) * lax.rsqrt(var + EPS) * lg_ref[...] + lb_ref[...]
    u = jax.nn.silu(uf) * jax.nn.silu(gate)
    x1 = x + jnp.dot(u.astype(BF16), wout_ref[...], preferred_element_type=F32)
    _from_frames(_rms(x1, fg_ref[...]), scr_ref, o_ref)


def _layer1(x3, g, win, cw, cb, lg, lb, wout, fg, tm):
    nb, _, nj, d = x3.shape
    width = wout.shape[0]
    jt = tm // S5_R
    nt = nj // jt
    full = lambda a: pl.BlockSpec(a.shape, lambda b, t: (0,) * a.ndim)
    n_shift = (CONV_K - 1 + S5_R - 1) // S5_R
    return pl.pallas_call(
        functools.partial(_layer1_kernel, jt=jt, width=width),
        grid=(nb, nt),
        in_specs=[pl.BlockSpec((1, S5_R, jt, d), lambda b, t: (b, 0, t, 0)),
                  full(g), full(win), full(cw), full(cb), full(lg), full(lb), full(wout), full(fg)],
        out_specs=pl.BlockSpec((tm, d), lambda b, t: (b * nt + t, 0)),
        out_shape=jax.ShapeDtypeStruct((nb * nj * S5_R, d), F32),
        scratch_shapes=[pltpu.VMEM((S5_R, CONV_CARRY + jt, width), F32),
                        pltpu.VMEM((n_shift, S5_R, jt, width), F32),
                        pltpu.VMEM((tm, width), F32),
                        pltpu.VMEM((d // LANES, tm, LANES), F32)],
        compiler_params=_cparams(("parallel", "arbitrary")),
        name="layer1",
    )(x3, g, win, cw, cb, lg, lb, wout, fg)


def _pick_tile(n, pref):
    t = min(n, pref)
    while n % t:
        t //= 2
    return t


def kernel(x, norm_g, final_g, w_in_ab, s5_a_re, s5_a_im, s5_log_dt, s5_b_re, s5_b_im, s5_c_re, s5_c_im,
           s5_d, s5_glu_w, s5_glu_b, w_out_ab, w_in_c, conv_w, conv_b, conv_ln_g, conv_ln_b, w_out_c):
    nb, seq, d = x.shape
    rows = nb * seq
    s5_w = s5_glu_w.shape[-1]
    G = s5_w // S5_GROUP
    ret_qk = RET_HEADS * RET_DK
    ret_w = RET_HEADS * RET_DV
    nj = seq // S5_R
    x2 = x.reshape(rows, d)

    w_in = w_in_ab[0].astype(BF16)
    tm = _pick_tile(seq, 512)
    q, k, v, gr3, hn3, x3 = _proj0(x2, norm_g[0:1], w_in[:, 2 * s5_w:], nb, seq, ret_qk, ret_w, tm)
    at, g5t = _proj_u(hn3, w_in[:, :2 * s5_w].T, s5_w, 4)

    dup = lambda a: jnp.concatenate([a, a], axis=-1)
    are2 = dup(s5_a_re[0])[:, None, :]
    aim2 = dup(s5_a_im[0])[:, None, :]
    ldt = s5_log_dt[0][:, None, None]
    bre2 = dup(jnp.swapaxes(s5_b_re[0], 1, 2))
    bim2 = dup(jnp.swapaxes(s5_b_im[0], 1, 2))
    cre2 = dup(s5_c_re[0])
    cim2 = dup(s5_c_im[0])
    dsk_t = jnp.tile(s5_d[0].reshape(G, 1, S5_GROUP), (1, S5_R, 1)).reshape(G, S5_BLK, 1)
    mt, bsre, bsim, ccat, a16 = _s5_prep(are2, aim2, ldt, bre2, bim2, cre2, cim2, dsk_t)

    y5t = _s5_core(at, mt, bsre, bsim, ccat, a16)

    pos = jnp.arange(seq, dtype=F32)[:, None]
    half = RET_DK // 2
    freqs = ROPE_BASE ** (-jnp.arange(half, dtype=F32) / half)
    ang = pos * freqs[None, :]
    cos_h = jnp.concatenate([jnp.cos(ang), jnp.cos(ang)], axis=-1)
    sin_h = jnp.concatenate([-jnp.sin(ang), jnp.sin(ang)], axis=-1)
    cos_t = jnp.tile(cos_h, (1, RET_HEADS))
    sin_t = jnp.tile(sin_h, (1, RET_HEADS))
    log_g = jnp.log(1.0 - 2.0 ** (-5.0 - jnp.arange(RET_HEADS, dtype=F32)))
    idx = jnp.arange(CHUNK, dtype=F32)
    intra = jnp.exp(log_g[:, None, None] * jnp.abs(idx[:, None] - idx[None, :]))
    qdec = jnp.exp(log_g[:, None] * (idx + 1.0)[None, :])[:, :, None]
    kdec = jnp.exp(log_g[:, None] * (CHUNK - 1 - idx)[None, :])[:, :, None]
    cdec = jnp.broadcast_to(jnp.exp(log_g * CHUNK)[:, None, None], (RET_HEADS, 1, RET_DV))
    yr3 = _retention(q, k, v, cos_t, sin_t, intra, qdec, kdec, cdec, nb, seq, tm)

    wo = w_out_ab[0].astype(BF16)
    x3 = _merge0(x3, y5t, g5t, yr3, gr3, s5_glu_w[0].T.astype(BF16), s5_glu_b[0][:, None],
                 wo[:s5_w], wo[s5_w:])

    out = _layer1(x3, norm_g[1:2], w_in_c[0].astype(BF16), conv_w[0], conv_b[0:1], conv_ln_g[0:1],
                  conv_ln_b[0:1], w_out_c[0].astype(BF16), final_g[None, :], tm)
    return out.reshape(nb, seq, d)
```

```python
import functools
import math

import jax
import jax.numpy as jnp
from jax import lax
from jax.experimental import pallas as pl
from jax.experimental.pallas import tpu as pltpu

F32 = jnp.float32
BF16 = jnp.bfloat16

EPS = 1e-6
CHUNK = 64
S5_GROUP = 16
S5_STATE = 64
S5_R = 16
S5_BLK = S5_R * S5_GROUP
RET_HEADS = 8
RET_DK = 64
RET_DV = 128
RET_BLK = 256
ROPE_BASE = 10000.0
CONV_K = 31
CONV_HALO = 32

LANES = 128
VMEM_LIMIT_BYTES = 56 * 1024 * 1024

_NT = (((1,), (1,)), ((), ()))
_TN = (((0,), (0,)), ((), ()))


def _cparams(sem):
    return pltpu.CompilerParams(dimension_semantics=sem, vmem_limit_bytes=VMEM_LIMIT_BYTES)


def _rms(x, g):
    return x * lax.rsqrt(jnp.mean(x * x, axis=-1, keepdims=True) + EPS) * g


def _to_frames(val, scr_ref, dst_ref):
    nc, rows, _ = scr_ref.shape
    for c in range(nc):
        scr_ref[c] = val[:, c * LANES:(c + 1) * LANES]
    for s in range(S5_R):
        dst_ref[0, s] = jnp.concatenate(
            [scr_ref[c, pl.ds(s, rows // S5_R, stride=S5_R), :] for c in range(nc)], axis=1).astype(dst_ref.dtype)


def _from_frames(val, scr_ref, dst_ref):
    nc, rows, _ = scr_ref.shape
    jt = rows // S5_R
    for s in range(S5_R):
        for c in range(nc):
            scr_ref[c, pl.ds(s, jt, stride=S5_R), :] = val[s * jt:(s + 1) * jt, c * LANES:(c + 1) * LANES]
    dst_ref[...] = jnp.concatenate([scr_ref[c] for c in range(nc)], axis=1)


def _proj0_kernel(x_ref, g_ref, w_ref, q_ref, k_ref, v_ref, gr3_ref, hn3_ref, x3_ref, scr_ref):
    x = x_ref[...]
    hn32 = _rms(x, g_ref[...])
    hn = hn32.astype(BF16)
    off = 0
    for ref in (q_ref, k_ref, v_ref):
        wd = ref.shape[-1]
        ref[...] = jnp.dot(hn, w_ref[:, off:off + wd], preferred_element_type=F32).astype(ref.dtype)
        off += wd
    _to_frames(hn32, scr_ref, hn3_ref)
    _to_frames(x, scr_ref, x3_ref)
    _to_frames(jnp.dot(hn, w_ref[:, off:], preferred_element_type=F32), scr_ref, gr3_ref)


def _proj0(x2, g, w, nb, seq, qk_w, v_w, tm):
    rows, d = x2.shape
    nt = seq // tm
    jt = tm // S5_R
    nj = seq // S5_R
    nat = lambda wd: pl.BlockSpec((tm, wd), lambda b, t: (b * nt + t, 0))
    fm = lambda wd: pl.BlockSpec((1, S5_R, jt, wd), lambda b, t: (b, 0, t, 0))
    fm_shape = lambda wd, dt: jax.ShapeDtypeStruct((nb, S5_R, nj, wd), dt)
    assert d == v_w, "the strided-read scratch is shared between hn and the retention gate"
    return pl.pallas_call(
        _proj0_kernel,
        grid=(nb, nt),
        in_specs=[nat(d), pl.BlockSpec((1, d), lambda b, t: (0, 0)), pl.BlockSpec(w.shape, lambda b, t: (0, 0))],
        out_specs=[nat(qk_w), nat(qk_w), nat(v_w), fm(v_w), fm(d), fm(d)],
        out_shape=[jax.ShapeDtypeStruct((rows, qk_w), BF16), jax.ShapeDtypeStruct((rows, qk_w), BF16),
                   jax.ShapeDtypeStruct((rows, v_w), BF16), fm_shape(v_w, BF16), fm_shape(d, BF16),
                   fm_shape(d, F32)],
        scratch_shapes=[pltpu.VMEM((d // LANES, tm, LANES), F32)],
        compiler_params=_cparams(("parallel", "parallel")),
        name="proj0",
    )(x2, g, w)


def _proj_u_kernel(hn3_ref, wt_ref, at_ref, g5t_ref, *, sg, s5_w):
    nj = hn3_ref.shape[2]
    for sl in range(sg):
        ut = lax.dot_general(wt_ref[...], hn3_ref[0, sl], _NT, preferred_element_type=F32)
        at_ref[0, :, sl * S5_GROUP:(sl + 1) * S5_GROUP, :] = (
            ut[:s5_w].reshape(s5_w // S5_GROUP, S5_GROUP, nj).astype(at_ref.dtype))
        g5t_ref[0, sl] = ut[s5_w:].astype(g5t_ref.dtype)


def _proj_u(hn3, wt, s5_w, sg):
    nb, _, nj, d = hn3.shape
    G = s5_w // S5_GROUP
    return pl.pallas_call(
        functools.partial(_proj_u_kernel, sg=sg, s5_w=s5_w),
        grid=(nb, S5_R // sg),
        in_specs=[
            pl.BlockSpec((1, sg, nj, d), lambda b, s: (b, s, 0, 0)),
            pl.BlockSpec(wt.shape, lambda b, s: (0, 0)),
        ],
        out_specs=[
            pl.BlockSpec((1, G, sg * S5_GROUP, nj), lambda b, s: (b, 0, s, 0)),
            pl.BlockSpec((1, sg, s5_w, nj), lambda b, s: (b, s, 0, 0)),
        ],
        out_shape=[jax.ShapeDtypeStruct((nb, G, S5_BLK, nj), BF16),
                   jax.ShapeDtypeStruct((nb, S5_R, s5_w, nj), BF16)],
        compiler_params=_cparams(("parallel", "parallel")),
        name="proj_u",
    )(hn3, wt)


def _s5_prep_kernel(are_ref, aim_ref, ldt_ref, bre_ref, bim_ref, cre_ref, cim_ref, dsk_ref,
                    mt_ref, bsre_ref, bsim_ref, ccat_ref, a16_ref):
    g = pl.program_id(0)
    are = are_ref[0]
    aim = aim_ref[0]
    dt = jnp.exp(ldt_ref[0])
    bre, bim = bre_ref[0], bim_ref[0]
    cre, cim = cre_ref[0], cim_ref[0]

    def lpow(e):
        mag = jnp.exp(are * dt * e)
        ang = aim * dt * e
        return mag * jnp.cos(ang), mag * jnp.sin(ang)

    one = jnp.ones((1, 1), F32)
    lb_re, lb_im = lpow(one)
    nr, ni = lb_re - 1.0, lb_im
    den = are * are + aim * aim
    coef_re = (nr * are + ni * aim) / den
    coef_im = (ni * are - nr * aim) / den
    bb_re = coef_re * bre - coef_im * bim
    bb_im = coef_re * bim + coef_im * bre

    p = lax.broadcasted_iota(jnp.int32, (S5_R, 1), 0).astype(F32)
    lane = lax.broadcasted_iota(jnp.int32, (1, LANES), 1)
    mine = (lane // S5_STATE) == (g % 2)
    low = lane < S5_STATE

    def outer(t_re, t_im, m_re, m_im):
        re = t_re[:, None, :] * m_re[None, :, :] - t_im[:, None, :] * m_im[None, :, :]
        im = t_re[:, None, :] * m_im[None, :, :] + t_im[:, None, :] * m_re[None, :, :]
        return re.reshape(S5_BLK, LANES), im.reshape(S5_BLK, LANES)

    half = float(S5_R // 2)
    e_re, e_im = lpow(p - half)
    f_re, f_im = lpow(half - p)
    p_re, p_im = outer(e_re, e_im, cre, cim)
    q_re, q_im = outer(f_re, f_im, bb_re, bb_im)
    q_re = jnp.where(low, q_re, 0.0)
    q_im = jnp.where(low, q_im, 0.0)
    mt = (lax.dot_general(p_re, q_re, _NT, precision=lax.Precision.HIGHEST, preferred_element_type=F32)
          - lax.dot_general(p_im, q_im, _NT, precision=lax.Precision.HIGHEST, preferred_element_type=F32))
    row = lax.broadcasted_iota(jnp.int32, (S5_BLK, S5_BLK), 0)
    col = lax.broadcasted_iota(jnp.int32, (S5_BLK, S5_BLK), 1)
    mt = jnp.where(row // S5_GROUP >= col // S5_GROUP, mt, 0.0)
    mt = mt + jnp.where(row == col, dsk_ref[0], 0.0)
    mt_ref[0] = mt.astype(mt_ref.dtype)

    g_re, g_im = lpow(float(S5_R - 1) - p)
    bs_re, bs_im = outer(g_re, g_im, bb_re, bb_im)
    bsre_ref[0] = jnp.where(mine, bs_re, 0.0).astype(bsre_ref.dtype)
    bsim_ref[0] = jnp.where(mine, bs_im, 0.0).astype(bsim_ref.dtype)

    h_re, h_im = lpow(p + 1.0)
    cs_re, cs_im = outer(h_re, h_im, cre, cim)
    ccat_ref[0, :, 0:LANES] = jnp.where(mine, cs_re, 0.0).astype(ccat_ref.dtype)
    ccat_ref[0, :, LANES:2 * LANES] = jnp.where(mine, -cs_im, 0.0).astype(ccat_ref.dtype)

    a_re, a_im = lpow(one * float(S5_R))
    rows8 = lax.broadcasted_iota(jnp.int32, (8, LANES), 0)
    a16_ref[0] = jnp.where(mine, jnp.where(rows8 == 0, a_re, jnp.where(rows8 == 1, a_im, 0.0)), 0.0)


def _s5_prep(are2, aim2, ldt, bre2, bim2, cre2, cim2, dsk_t):
    G = are2.shape[0]
    spec3 = lambda shape: pl.BlockSpec((1,) + shape, lambda g: (g, 0, 0))
    return pl.pallas_call(
        _s5_prep_kernel,
        grid=(G,),
        in_specs=[spec3((1, LANES)), spec3((1, LANES)), spec3((1, 1)),
                  spec3((S5_GROUP, LANES)), spec3((S5_GROUP, LANES)),
                  spec3((S5_GROUP, LANES)), spec3((S5_GROUP, LANES)), spec3((S5_BLK, 1))],
        out_specs=[spec3((S5_BLK, S5_BLK)), spec3((S5_BLK, LANES)), spec3((S5_BLK, LANES)),
                   spec3((S5_BLK, 2 * LANES)), spec3((8, LANES))],
        out_shape=[jax.ShapeDtypeStruct((G, S5_BLK, S5_BLK), BF16),
                   jax.ShapeDtypeStruct((G, S5_BLK, LANES), BF16),
                   jax.ShapeDtypeStruct((G, S5_BLK, LANES), BF16),
                   jax.ShapeDtypeStruct((G, S5_BLK, 2 * LANES), BF16),
                   jax.ShapeDtypeStruct((G, 8, LANES), F32)],
        compiler_params=_cparams(("parallel",)),
        name="s5_prep",
    )(are2, aim2, ldt, bre2, bim2, cre2, cim2, dsk_t)


def _s5_core_kernel(at_ref, mt_ref, bsre_ref, bsim_ref, ccat_ref, a16_ref, yt_ref,
                    zre_ref, zim_ref, xre_ref, xim_ref, *, nb, nj):
    for b in range(nb):
        rows = pl.ds(b * nj, nj)
        zre_ref[rows, :] = (lax.dot_general(at_ref[b, 0], bsre_ref[0], _TN, preferred_element_type=F32)
                            + lax.dot_general(at_ref[b, 1], bsre_ref[1], _TN, preferred_element_type=F32))
        zim_ref[rows, :] = (lax.dot_general(at_ref[b, 0], bsim_ref[0], _TN, preferred_element_type=F32)
                            + lax.dot_general(at_ref[b, 1], bsim_ref[1], _TN, preferred_element_type=F32))
    a_re = jnp.broadcast_to(a16_ref[0, 0:1, :] + a16_ref[1, 0:1, :], (nb, LANES))
    a_im = jnp.broadcast_to(a16_ref[0, 1:2, :] + a16_ref[1, 1:2, :], (nb, LANES))

    def step(j, carry):
        xr, xi = carry
        idx = pl.ds(j, nb, stride=nj)
        xre_ref[idx, :] = xr
        xim_ref[idx, :] = xi
        zr = zre_ref[idx, :]
        zi = zim_ref[idx, :]
        return a_re * xr - a_im * xi + zr, a_re * xi + a_im * xr + zi

    zero = jnp.zeros((nb, LANES), F32)
    lax.fori_loop(0, nj, step, (zero, zero), unroll=8)
    for b in range(nb):
        rows = pl.ds(b * nj, nj)
        xcat = jnp.concatenate([xre_ref[rows, :], xim_ref[rows, :]], axis=1).astype(BF16)
        for i in range(2):
            yt = (jnp.dot(mt_ref[i], at_ref[b, i], preferred_element_type=F32)
                  + lax.dot_general(ccat_ref[i], xcat, _NT, preferred_element_type=F32))
            yt_ref[b, i] = yt.astype(yt_ref.dtype)


def _s5_core(at, mt, bsre, bsim, ccat, a16):
    nb, G, _, nj = at.shape
    pair = lambda shape: pl.BlockSpec((2,) + shape, lambda p: (p, 0, 0))
    data = pl.BlockSpec((nb, 2, S5_BLK, nj), lambda p: (0, p, 0, 0))
    return pl.pallas_call(
        functools.partial(_s5_core_kernel, nb=nb, nj=nj),
        grid=(G // 2,),
        in_specs=[data, pair((S5_BLK, S5_BLK)), pair((S5_BLK, LANES)),
                  pair((S5_BLK, LANES)), pair((S5_BLK, 2 * LANES)), pair((8, LANES))],
        out_specs=data,
        out_shape=jax.ShapeDtypeStruct(at.shape, BF16),
        scratch_shapes=[pltpu.VMEM((nb * nj, LANES), F32)] * 4,
        compiler_params=_cparams(("parallel",)),
        name="s5_core",
    )(at, mt, bsre, bsim, ccat, a16)


def _ret_kernel(q_ref, k_ref, v_ref, cos_ref, sin_ref, dm_ref, qdec_ref, kdec_ref, cdec_ref,
                o3_ref, s_ref, qs_ref, ks_ref, o_ref, *, n_blocks):
    @pl.when(pl.program_id(1) == 0)
    def _():
        s_ref[...] = jnp.zeros(s_ref.shape, s_ref.dtype)

    cos = cos_ref[...]
    sin = sin_ref[...]
    width = q_ref.shape[-1]
    lane = lax.broadcasted_iota(jnp.int32, cos.shape, 1)
    first = (lane % RET_DK) < (RET_DK // 2)

    def rope(x):
        partner = jnp.where(first, pltpu.roll(x, width - RET_DK // 2, 1), pltpu.roll(x, RET_DK // 2, 1))
        return x * cos + partner * sin

    qs_ref[...] = (rope(q_ref[...].astype(F32)) * (RET_DK ** -0.5)).astype(BF16)
    ks_ref[...] = rope(k_ref[...].astype(F32)).astype(BF16)

    lane_p = lax.broadcasted_iota(jnp.int32, (RET_BLK, LANES), 1)
    heads = range(RET_HEADS)
    for c in range(n_blocks):
        rows = pl.ds(c * RET_BLK, RET_BLK)
        qp = [qs_ref[rows, pl.ds((h // 2) * LANES, LANES)] for h in heads]
        km = [jnp.where((lane_p // RET_DK) == (h % 2), ks_ref[rows, pl.ds((h // 2) * LANES, LANES)],
                        jnp.zeros((RET_BLK, LANES), BF16)) for h in heads]
        vh = [v_ref[rows, pl.ds(h * RET_DV, RET_DV)] for h in heads]
        sc = [lax.dot_general(qp[h], km[h], _NT, preferred_element_type=F32) for h in heads]
        s_prev = [s_ref[h] for h in heads]
        cross = [jnp.dot(qp[h], s_prev[h].astype(BF16), preferred_element_type=F32) for h in heads]
        p = [(sc[h] * dm_ref[h]).astype(BF16) for h in heads]
        o = [jnp.dot(p[h], vh[h], preferred_element_type=F32) + qdec_ref[h] * cross[h] for h in heads]
        vd = [(vh[h].astype(F32) * kdec_ref[h]).astype(BF16) for h in heads]
        kv = [lax.dot_general(km[h], vd[h], _TN, preferred_element_type=F32) for h in heads]
        for h in heads:
            s_ref[h] = cdec_ref[h] * s_prev[h] + kv[h]
        mu = [jnp.mean(o[h], axis=-1, keepdims=True) for h in heads]
        var = [jnp.mean(jnp.square(o[h] - mu[h]), axis=-1, keepdims=True) for h in heads]
        for h in heads:
            o_ref[h, rows, :] = (o[h] - mu[h]) * lax.rsqrt(var[h] + EPS)
    jt = o3_ref.shape[2]
    for s in range(S5_R):
        o3_ref[0, s] = jnp.concatenate(
            [o_ref[h, pl.ds(s, jt, stride=S5_R), :] for h in range(RET_HEADS)], axis=1).astype(o3_ref.dtype)


def _retention(q, k, v, cos, sin, dm, qdec, kdec, cdec, nb, seq, tb):
    assert tb % RET_BLK == 0
    nt = seq // tb
    row_blk = lambda wd: pl.BlockSpec((tb, wd), lambda b, t: (b * nt + t, 0))
    pos_blk = lambda wd: pl.BlockSpec((tb, wd), lambda b, t: (t, 0))
    full = lambda a: pl.BlockSpec(a.shape, lambda b, t: (0,) * a.ndim)
    qk_w, v_w = q.shape[-1], v.shape[-1]
    return pl.pallas_call(
        functools.partial(_ret_kernel, n_blocks=tb // RET_BLK),
        grid=(nb, nt),
        in_specs=[row_blk(qk_w), row_blk(qk_w), row_blk(v_w), pos_blk(qk_w), pos_blk(qk_w),
                  full(dm), full(qdec), full(kdec), full(cdec)],
        out_specs=pl.BlockSpec((1, S5_R, tb // S5_R, v_w), lambda b, t: (b, 0, t, 0)),
        out_shape=jax.ShapeDtypeStruct((nb, S5_R, seq // S5_R, v_w), BF16),
        scratch_shapes=[pltpu.VMEM((RET_HEADS, LANES, RET_DV), F32),
                        pltpu.VMEM((tb, qk_w), BF16), pltpu.VMEM((tb, qk_w), BF16),
                        pltpu.VMEM((RET_HEADS, tb, RET_DV), F32)],
        compiler_params=_cparams(("parallel", "arbitrary")),
        name="retention",
    )(q, k, v, cos, sin, dm, qdec, kdec, cdec)


MERGE_CB = 256
def _merge0_kernel(x_ref, y5t_ref, g5t_ref, yr_ref, gr_ref, gwt_ref, gb_ref, wo5_ref, wor_ref, o_ref):
    nj = x_ref.shape[1]
    s5_w = gwt_ref.shape[0]
    y = jax.nn.gelu(y5t_ref[0].astype(F32).reshape(-1, nj))
    yb = y.astype(BF16)
    r = yr_ref[0].astype(F32) * jax.nn.silu(gr_ref[0].astype(F32))
    blocks = [pl.ds(c0, MERGE_CB) for c0 in range(0, s5_w, MERGE_CB)]
    gates = [jnp.dot(gwt_ref[cb, :], yb, preferred_element_type=F32) + gb_ref[cb, :] for cb in blocks]
    out = x_ref[0] + jnp.dot(r.astype(BF16), wor_ref[...], preferred_element_type=F32)
    for n, cb in enumerate(blocks):
        a = (y[n * MERGE_CB:(n + 1) * MERGE_CB] * jax.nn.sigmoid(gates[n])
             * jax.nn.silu(g5t_ref[0, cb, :].astype(F32)))
        out = out + lax.dot_general(a.astype(BF16), wo5_ref[cb, :], _TN, preferred_element_type=F32)
    o_ref[0] = out


def _merge0(x3, y5t, g5t, yr3, gr3, gwt, gb_col, wo5, wor):
    nb, _, nj, d = x3.shape
    G = y5t.shape[1]
    s5_w = g5t.shape[2]
    ret_w = wor.shape[0]
    y5t5 = y5t.reshape(nb, G, S5_R, S5_GROUP, nj)
    strided = lambda wd: pl.BlockSpec((1, None, nj, wd), lambda b, s: (b, s, 0, 0))
    full = lambda a: pl.BlockSpec(a.shape, lambda b, s: (0,) * a.ndim)
    return pl.pallas_call(
        _merge0_kernel,
        grid=(nb, S5_R),
        in_specs=[strided(d),
                  pl.BlockSpec((1, G, None, S5_GROUP, nj), lambda b, s: (b, 0, s, 0, 0)),
                  pl.BlockSpec((1, None, s5_w, nj), lambda b, s: (b, s, 0, 0)),
                  strided(ret_w), strided(ret_w), full(gwt), full(gb_col), full(wo5), full(wor)],
        out_specs=strided(d),
        out_shape=jax.ShapeDtypeStruct(x3.shape, F32),
        compiler_params=_cparams(("parallel", "parallel")),
        name="merge0",
    )(x3, y5t5, g5t, yr3, gr3, gwt, gb_col, wo5, wor)


CONV_CARRY = 8
CONV_TB = 8
MM_COLS = 256
LOOP_TRIPS = 4
assert (CONV_K - 1 + S5_R - 1) // S5_R <= CONV_CARRY


def _layer1_kernel(xa_ref, xc_ref, g_ref, win_ref, cw_ref, cb_ref, lg_ref, lb_ref, wout_ref, fg_ref, o_ref,
                   ubuf0_ref, ubuf1_ref, gate0_ref, gate1_ref, sh_ref, acc_ref, scr_ref,
                   hn_ref, pr_ref, u_ref, od_ref, *, jt, width, nt):
    i = pl.program_id(0)
    rows = S5_R * jt
    d = xa_ref.shape[-1]
    n_shift = sh_ref.shape[1]
    n_strips = acc_ref.shape[0]
    n_proj, n_out = win_ref.shape[0], wout_ref.shape[0]
    n_ab = 2 * width // MM_COLS
    assert n_strips % LOOP_TRIPS == 0 and n_proj % LOOP_TRIPS == 0 and n_out == LOOP_TRIPS

    @pl.when(i == 0)
    def _():
        ubuf1_ref[...] = jnp.zeros(ubuf1_ref.shape, F32)
        gate1_ref[...] = jnp.zeros(gate1_ref.shape, F32)
        u_ref[...] = jnp.zeros(u_ref.shape, u_ref.dtype)

    def conv_strip(ubuf_b, c):
        for dl in range(1, n_shift + 1):
            sh_ref[c, dl - 1] = ubuf_b[c, :, CONV_CARRY - dl:CONV_CARRY - dl + jt, :]
        for s0 in range(0, S5_R, CONV_TB):
            targets = range(s0, s0 + CONV_TB)
            accs = {s: jnp.broadcast_to(cb_ref[c], (jt, LANES)) for s in targets}
            for e in range(s0 - (CONV_K - 1), s0 + CONV_TB):
                plane, dl = e % S5_R, -(e // S5_R)
                src = ubuf_b[c, plane, CONV_CARRY:CONV_CARRY + jt, :] if dl == 0 else sh_ref[c, dl - 1, plane]
                for s in targets:
                    kk = e + (CONV_K - 1) - s
                    if 0 <= kk < CONV_K:
                        accs[s] = accs[s] + src * cw_ref[c, kk:kk + 1, :]
            for s in targets:
                acc_ref[c, s * jt:(s + 1) * jt, :] = accs[s]

    def step(ubuf_a, gate_a, ubuf_b, gate_b):
        hn_ref[...] = _rms(xa_ref[0].reshape(rows, d), g_ref[...]).astype(BF16)

        def body(it, carry):
            for k in range(n_strips // LOOP_TRIPS):
                conv_strip(ubuf_b, it * (n_strips // LOOP_TRIPS) + k)
            for k in range(n_proj // LOOP_TRIPS):
                n = it * (n_proj // LOOP_TRIPS) + k
                pr_ref[n] = jnp.dot(hn_ref[...], win_ref[n], preferred_element_type=F32)
            od_ref[it] = jnp.dot(u_ref[...], wout_ref[it], preferred_element_type=F32)
            return carry

        lax.fori_loop(0, LOOP_TRIPS, body, 0)

        x1 = xc_ref[0].reshape(rows, d) + jnp.concatenate([od_ref[n] for n in range(n_out)], axis=1)
        _from_frames(_rms(x1, fg_ref[...]), scr_ref, o_ref)

        acc = jnp.concatenate([acc_ref[c] for c in range(n_strips)], axis=1)
        mu = jnp.mean(acc, axis=-1, keepdims=True)
        var = jnp.mean(jnp.square(acc - mu), axis=-1, keepdims=True)
        uf = (acc - mu) * lax.rsqrt(var + EPS) * lg_ref[...] + lb_ref[...]
        gate = jnp.concatenate([gate_b[n] for n in range(gate_b.shape[0])], axis=1)
        u_ref[...] = (jax.nn.silu(uf) * jax.nn.silu(gate)).astype(u_ref.dtype)

        first = (i % nt) == 0
        for n in range(n_proj - n_ab):
            gate_a[n] = pr_ref[n_ab + n]
        for n in range(n_ab // 2):
            glu = pr_ref[n] * jax.nn.sigmoid(pr_ref[n_ab // 2 + n])
            for k in range(MM_COLS // LANES):
                c = n * (MM_COLS // LANES) + k
                ubuf_a[c, :, CONV_CARRY:CONV_CARRY + jt, :] = glu[:, k * LANES:(k + 1) * LANES].reshape(S5_R, jt, LANES)
        for c in range(n_strips):
            hist = ubuf_b[c, :, jt:jt + CONV_CARRY, :]
            ubuf_a[c, :, 0:CONV_CARRY, :] = jnp.where(first, jnp.zeros_like(hist), hist)

    @pl.when(i % 2 == 0)
    def _():
        step(ubuf0_ref, gate0_ref, ubuf1_ref, gate1_ref)

    @pl.when(i % 2 == 1)
    def _():
        step(ubuf1_ref, gate1_ref, ubuf0_ref, gate0_ref)


def _layer1(x3, g, win, cw, cb, lg, lb, wout, fg, tm):
    nb, _, nj, d = x3.shape
    width = wout.shape[0]
    jt = tm // S5_R
    nt = nj // jt
    n_tiles = nb * nt
    n_strips = width // LANES
    n_shift = (CONV_K - 1 + S5_R - 1) // S5_R
    win3 = win.reshape(d, 3 * width // MM_COLS, MM_COLS).transpose(1, 0, 2)
    wout3 = wout.reshape(width, d // MM_COLS, MM_COLS).transpose(1, 0, 2)
    cw3 = cw.reshape(CONV_K, n_strips, LANES).transpose(1, 0, 2)
    cb3 = cb.reshape(n_strips, 1, LANES)
    full = lambda a: pl.BlockSpec(a.shape, lambda i: (0,) * a.ndim, pipeline_mode=pl.Buffered(1))
    tile_a = lambda i: jnp.minimum(i, n_tiles - 1)
    tile_c = lambda i: jnp.clip(i - 2, 0, n_tiles - 1)
    plane_blk = lambda tile: pl.BlockSpec((1, S5_R, jt, d), lambda i: (tile(i) // nt, 0, tile(i) % nt, 0))
    ubuf = pltpu.VMEM((n_strips, S5_R, CONV_CARRY + jt, LANES), F32)
    gate = pltpu.VMEM((width // MM_COLS, tm, MM_COLS), F32)
    return pl.pallas_call(
        functools.partial(_layer1_kernel, jt=jt, width=width, nt=nt),
        grid=(n_tiles + 2,),
        in_specs=[plane_blk(tile_a), plane_blk(tile_c),
                  full(g), full(win3), full(cw3), full(cb3), full(lg), full(lb), full(wout3), full(fg)],
        out_specs=pl.BlockSpec((tm, d), lambda i: (tile_c(i), 0)),
        out_shape=jax.ShapeDtypeStruct((nb * nj * S5_R, d), F32),
        scratch_shapes=[ubuf, ubuf, gate, gate,
                        pltpu.VMEM((n_strips, n_shift, S5_R, jt, LANES), F32),
                        pltpu.VMEM((n_strips, tm, LANES), F32),
                        pltpu.VMEM((d // LANES, tm, LANES), F32),
                        pltpu.VMEM((tm, d), BF16),
                        pltpu.VMEM((3 * width // MM_COLS, tm, MM_COLS), F32),
                        pltpu.VMEM((tm, width), BF16),
                        pltpu.VMEM((d // MM_COLS, tm, MM_COLS), F32)],
        compiler_params=_cparams(("arbitrary",)),
        name="layer1",
    )(x3, x3, g, win3, cw3, cb3, lg, lb, wout3, fg)


def _pick_tile(n, pref):
    t = min(n, pref)
    while n % t:
        t //= 2
    return t


def kernel(x, norm_g, final_g, w_in_ab, s5_a_re, s5_a_im, s5_log_dt, s5_b_re, s5_b_im, s5_c_re, s5_c_im,
           s5_d, s5_glu_w, s5_glu_b, w_out_ab, w_in_c, conv_w, conv_b, conv_ln_g, conv_ln_b, w_out_c):
    nb, seq, d = x.shape
    rows = nb * seq
    s5_w = s5_glu_w.shape[-1]
    G = s5_w // S5_GROUP
    ret_qk = RET_HEADS * RET_DK
    ret_w = RET_HEADS * RET_DV
    nj = seq // S5_R
    x2 = x.reshape(rows, d)

    w_in = w_in_ab[0].astype(BF16)
    tm = _pick_tile(seq, 512)
    q, k, v, gr3, hn3, x3 = _proj0(x2, norm_g[0:1], w_in[:, 2 * s5_w:], nb, seq, ret_qk, ret_w, tm)
    at, g5t = _proj_u(hn3, w_in[:, :2 * s5_w].T, s5_w, 4)

    dup = lambda a: jnp.concatenate([a, a], axis=-1)
    are2 = dup(s5_a_re[0])[:, None, :]
    aim2 = dup(s5_a_im[0])[:, None, :]
    ldt = s5_log_dt[0][:, None, None]
    bre2 = dup(jnp.swapaxes(s5_b_re[0], 1, 2))
    bim2 = dup(jnp.swapaxes(s5_b_im[0], 1, 2))
    cre2 = dup(s5_c_re[0])
    cim2 = dup(s5_c_im[0])
    dsk_t = jnp.tile(s5_d[0].reshape(G, 1, S5_GROUP), (1, S5_R, 1)).reshape(G, S5_BLK, 1)
    mt, bsre, bsim, ccat, a16 = _s5_prep(are2, aim2, ldt, bre2, bim2, cre2, cim2, dsk_t)

    y5t = _s5_core(at, mt, bsre, bsim, ccat, a16)

    pos = jnp.arange(seq, dtype=F32)[:, None]
    half = RET_DK // 2
    freqs = ROPE_BASE ** (-jnp.arange(half, dtype=F32) / half)
    ang = pos * freqs[None, :]
    cos_h = jnp.concatenate([jnp.cos(ang), jnp.cos(ang)], axis=-1)
    sin_h = jnp.concatenate([-jnp.sin(ang), jnp.sin(ang)], axis=-1)
    cos_t = jnp.tile(cos_h, (1, RET_HEADS))
    sin_t = jnp.tile(sin_h, (1, RET_HEADS))
    log_g = jnp.log(1.0 - 2.0 ** (-5.0 - jnp.arange(RET_HEADS, dtype=F32)))
    idx = jnp.arange(RET_BLK, dtype=F32)
    chunk_of = jnp.arange(RET_BLK) // CHUNK
    seen = chunk_of[:, None] >= chunk_of[None, :]
    dm = jnp.where(seen[None], jnp.exp(log_g[:, None, None] * jnp.abs(idx[:, None] - idx[None, :])[None]), 0.0)
    qdec = jnp.exp(log_g[:, None] * (idx + 1.0)[None, :])[:, :, None]
    kdec = jnp.exp(log_g[:, None] * (RET_BLK - 1 - idx)[None, :])[:, :, None]
    cdec = jnp.broadcast_to(jnp.exp(log_g * RET_BLK)[:, None, None], (RET_HEADS, 1, RET_DV))
    yr3 = _retention(q, k, v, cos_t, sin_t, dm, qdec, kdec, cdec, nb, seq, tm)

    wo = w_out_ab[0].astype(BF16)
    x3 = _merge0(x3, y5t, g5t, yr3, gr3, s5_glu_w[0].T.astype(BF16), s5_glu_b[0][:, None],
                 wo[:s5_w], wo[s5_w:])

    out = _layer1(x3, norm_g[1:2], w_in_c[0].astype(BF16), conv_w[0], conv_b[0:1], conv_ln_g[0:1],
                  conv_ln_b[0:1], w_out_c[0].astype(BF16), final_g[None, :], tm)
    return out.reshape(nb, seq, d)
```

```python
import functools
import math

import jax
import jax.numpy as jnp
from jax import lax
from jax.experimental import pallas as pl
from jax.experimental.pallas import tpu as pltpu

F32 = jnp.float32
BF16 = jnp.bfloat16

EPS = 1e-6
CHUNK = 64
S5_GROUP = 16
S5_STATE = 64
S5_R = 16
S5_BLK = S5_R * S5_GROUP
RET_HEADS = 8
RET_DK = 64
RET_DV = 128
RET_BLK = 256
ROPE_BASE = 10000.0
CONV_K = 31
CONV_HALO = 32

LANES = 128
VMEM_LIMIT_BYTES = 56 * 1024 * 1024

_NT = (((1,), (1,)), ((), ()))
_TN = (((0,), (0,)), ((), ()))


def _cparams(sem):
    return pltpu.CompilerParams(dimension_semantics=sem, vmem_limit_bytes=VMEM_LIMIT_BYTES)


def _rms(x, g):
    return x * lax.rsqrt(jnp.mean(x * x, axis=-1, keepdims=True) + EPS) * g


FRAME_PITCH = 24
assert FRAME_PITCH >= S5_R and FRAME_PITCH % 8 == 0


def _frame_scratch(width, rows):
    return pltpu.VMEM((width // LANES, rows // S5_R * FRAME_PITCH, LANES), F32)


def _to_frames(val, scr_ref, dst_ref):
    nc = scr_ref.shape[0]
    jt = scr_ref.shape[1] // FRAME_PITCH
    for c in range(nc):
        for j in range(jt):
            scr_ref[c, j * FRAME_PITCH:j * FRAME_PITCH + S5_R, :] = val[j * S5_R:(j + 1) * S5_R, c * LANES:(c + 1) * LANES]
    for s in range(S5_R):
        dst_ref[0, s] = jnp.concatenate(
            [scr_ref[c, pl.ds(s, jt, stride=FRAME_PITCH), :] for c in range(nc)], axis=1).astype(dst_ref.dtype)


def _from_frames(val, scr_ref, dst_ref):
    nc = scr_ref.shape[0]
    jt = scr_ref.shape[1] // FRAME_PITCH
    for s in range(S5_R):
        for c in range(nc):
            scr_ref[c, pl.ds(s, jt, stride=FRAME_PITCH), :] = val[s * jt:(s + 1) * jt, c * LANES:(c + 1) * LANES]
    for j in range(jt):
        dst_ref[j * S5_R:(j + 1) * S5_R, :] = jnp.concatenate(
            [scr_ref[c, j * FRAME_PITCH:j * FRAME_PITCH + S5_R, :] for c in range(nc)], axis=1)


def _proj0_kernel(x_ref, g_ref, w_ref, q_ref, k_ref, v_ref, gr3_ref, hn3_ref, x3_ref, scr_ref):
    x = x_ref[...]
    hn32 = _rms(x, g_ref[...])
    hn = hn32.astype(BF16)
    off = 0
    for ref in (q_ref, k_ref, v_ref):
        wd = ref.shape[-1]
        ref[...] = jnp.dot(hn, w_ref[:, off:off + wd], preferred_element_type=F32).astype(ref.dtype)
        off += wd
    _to_frames(hn32, scr_ref, hn3_ref)
    _to_frames(x, scr_ref, x3_ref)
    _to_frames(jnp.dot(hn, w_ref[:, off:], preferred_element_type=F32), scr_ref, gr3_ref)


def _proj0(x2, g, w, nb, seq, qk_w, v_w, tm):
    rows, d = x2.shape
    nt = seq // tm
    jt = tm // S5_R
    nj = seq // S5_R
    nat = lambda wd: pl.BlockSpec((tm, wd), lambda b, t: (b * nt + t, 0))
    fm = lambda wd: pl.BlockSpec((1, S5_R, jt, wd), lambda b, t: (b, 0, t, 0))
    fm_shape = lambda wd, dt: jax.ShapeDtypeStruct((nb, S5_R, nj, wd), dt)
    assert d == v_w, "the strided-read scratch is shared between hn and the retention gate"
    return pl.pallas_call(
        _proj0_kernel,
        grid=(nb, nt),
        in_specs=[nat(d), pl.BlockSpec((1, d), lambda b, t: (0, 0)), pl.BlockSpec(w.shape, lambda b, t: (0, 0))],
        out_specs=[nat(qk_w), nat(qk_w), nat(v_w), fm(v_w), fm(d), fm(d)],
        out_shape=[jax.ShapeDtypeStruct((rows, qk_w), BF16), jax.ShapeDtypeStruct((rows, qk_w), BF16),
                   jax.ShapeDtypeStruct((rows, v_w), BF16), fm_shape(v_w, BF16), fm_shape(d, BF16),
                   fm_shape(d, F32)],
        scratch_shapes=[_frame_scratch(d, tm)],
        compiler_params=_cparams(("parallel", "parallel")),
        name="proj0",
    )(x2, g, w)


def _proj_u_kernel(hn3_ref, wt_ref, at_ref, g5t_ref, *, sg, s5_w):
    nj = hn3_ref.shape[2]
    for sl in range(sg):
        ut = lax.dot_general(wt_ref[...], hn3_ref[0, sl], _NT, preferred_element_type=F32)
        at_ref[0, :, sl * S5_GROUP:(sl + 1) * S5_GROUP, :] = (
            ut[:s5_w].reshape(s5_w // S5_GROUP, S5_GROUP, nj).astype(at_ref.dtype))
        g5t_ref[0, sl] = ut[s5_w:].astype(g5t_ref.dtype)


def _proj_u(hn3, wt, s5_w, sg):
    nb, _, nj, d = hn3.shape
    G = s5_w // S5_GROUP
    return pl.pallas_call(
        functools.partial(_proj_u_kernel, sg=sg, s5_w=s5_w),
        grid=(nb, S5_R // sg),
        in_specs=[
            pl.BlockSpec((1, sg, nj, d), lambda b, s: (b, s, 0, 0)),
            pl.BlockSpec(wt.shape, lambda b, s: (0, 0)),
        ],
        out_specs=[
            pl.BlockSpec((1, G, sg * S5_GROUP, nj), lambda b, s: (b, 0, s, 0)),
            pl.BlockSpec((1, sg, s5_w, nj), lambda b, s: (b, s, 0, 0)),
        ],
        out_shape=[jax.ShapeDtypeStruct((nb, G, S5_BLK, nj), BF16),
                   jax.ShapeDtypeStruct((nb, S5_R, s5_w, nj), BF16)],
        compiler_params=_cparams(("parallel", "parallel")),
        name="proj_u",
    )(hn3, wt)


def _s5_prep_kernel(are_ref, aim_ref, ldt_ref, bre_ref, bim_ref, cre_ref, cim_ref, dsk_ref,
                    mt_ref, bsre_ref, bsim_ref, ccat_ref, a16_ref):
    g = pl.program_id(0)
    are = are_ref[0]
    aim = aim_ref[0]
    dt = jnp.exp(ldt_ref[0])
    bre, bim = bre_ref[0], bim_ref[0]
    cre, cim = cre_ref[0], cim_ref[0]

    def lpow(e):
        mag = jnp.exp(are * dt * e)
        ang = aim * dt * e
        return mag * jnp.cos(ang), mag * jnp.sin(ang)

    one = jnp.ones((1, 1), F32)
    lb_re, lb_im = lpow(one)
    nr, ni = lb_re - 1.0, lb_im
    den = are * are + aim * aim
    coef_re = (nr * are + ni * aim) / den
    coef_im = (ni * are - nr * aim) / den
    bb_re = coef_re * bre - coef_im * bim
    bb_im = coef_re * bim + coef_im * bre

    p = lax.broadcasted_iota(jnp.int32, (S5_R, 1), 0).astype(F32)
    lane = lax.broadcasted_iota(jnp.int32, (1, LANES), 1)
    mine = (lane // S5_STATE) == (g % 2)
    low = lane < S5_STATE

    def outer(t_re, t_im, m_re, m_im):
        re = t_re[:, None, :] * m_re[None, :, :] - t_im[:, None, :] * m_im[None, :, :]
        im = t_re[:, None, :] * m_im[None, :, :] + t_im[:, None, :] * m_re[None, :, :]
        return re.reshape(S5_BLK, LANES), im.reshape(S5_BLK, LANES)

    half = float(S5_R // 2)
    e_re, e_im = lpow(p - half)
    f_re, f_im = lpow(half - p)
    p_re, p_im = outer(e_re, e_im, cre, cim)
    q_re, q_im = outer(f_re, f_im, bb_re, bb_im)
    q_re = jnp.where(low, q_re, 0.0)
    q_im = jnp.where(low, q_im, 0.0)
    mt = (lax.dot_general(p_re, q_re, _NT, precision=lax.Precision.HIGHEST, preferred_element_type=F32)
          - lax.dot_general(p_im, q_im, _NT, precision=lax.Precision.HIGHEST, preferred_element_type=F32))
    row = lax.broadcasted_iota(jnp.int32, (S5_BLK, S5_BLK), 0)
    col = lax.broadcasted_iota(jnp.int32, (S5_BLK, S5_BLK), 1)
    mt = jnp.where(row // S5_GROUP >= col // S5_GROUP, mt, 0.0)
    mt = mt + jnp.where(row == col, dsk_ref[0], 0.0)
    mt_ref[0] = mt.astype(mt_ref.dtype)

    g_re, g_im = lpow(float(S5_R - 1) - p)
    bs_re, bs_im = outer(g_re, g_im, bb_re, bb_im)
    bsre_ref[0] = jnp.where(mine, bs_re, 0.0).astype(bsre_ref.dtype)
    bsim_ref[0] = jnp.where(mine, bs_im, 0.0).astype(bsim_ref.dtype)

    h_re, h_im = lpow(p + 1.0)
    cs_re, cs_im = outer(h_re, h_im, cre, cim)
    ccat_ref[0, :, 0:LANES] = jnp.where(mine, cs_re, 0.0).astype(ccat_ref.dtype)
    ccat_ref[0, :, LANES:2 * LANES] = jnp.where(mine, -cs_im, 0.0).astype(ccat_ref.dtype)

    a_re, a_im = lpow(one * float(S5_R))
    rows8 = lax.broadcasted_iota(jnp.int32, (8, LANES), 0)
    a16_ref[0] = jnp.where(mine, jnp.where(rows8 == 0, a_re, jnp.where(rows8 == 1, a_im, 0.0)), 0.0)


def _s5_prep(are2, aim2, ldt, bre2, bim2, cre2, cim2, dsk_t):
    G = are2.shape[0]
    spec3 = lambda shape: pl.BlockSpec((1,) + shape, lambda g: (g, 0, 0))
    return pl.pallas_call(
        _s5_prep_kernel,
        grid=(G,),
        in_specs=[spec3((1, LANES)), spec3((1, LANES)), spec3((1, 1)),
                  spec3((S5_GROUP, LANES)), spec3((S5_GROUP, LANES)),
                  spec3((S5_GROUP, LANES)), spec3((S5_GROUP, LANES)), spec3((S5_BLK, 1))],
        out_specs=[spec3((S5_BLK, S5_BLK)), spec3((S5_BLK, LANES)), spec3((S5_BLK, LANES)),
                   spec3((S5_BLK, 2 * LANES)), spec3((8, LANES))],
        out_shape=[jax.ShapeDtypeStruct((G, S5_BLK, S5_BLK), BF16),
                   jax.ShapeDtypeStruct((G, S5_BLK, LANES), BF16),
                   jax.ShapeDtypeStruct((G, S5_BLK, LANES), BF16),
                   jax.ShapeDtypeStruct((G, S5_BLK, 2 * LANES), BF16),
                   jax.ShapeDtypeStruct((G, 8, LANES), F32)],
        compiler_params=_cparams(("parallel",)),
        name="s5_prep",
    )(are2, aim2, ldt, bre2, bim2, cre2, cim2, dsk_t)


def _s5_core_kernel(at_ref, mt_ref, bsre_ref, bsim_ref, ccat_ref, a16_ref, yt_ref,
                    zre_ref, zim_ref, xre_ref, xim_ref, yloc_ref, *, nb, nj):
    for b in range(nb):
        rows = pl.ds(b * nj, nj)
        zre_ref[rows, :] = (lax.dot_general(at_ref[b, 0], bsre_ref[0], _TN, preferred_element_type=F32)
                            + lax.dot_general(at_ref[b, 1], bsre_ref[1], _TN, preferred_element_type=F32))
        zim_ref[rows, :] = (lax.dot_general(at_ref[b, 0], bsim_ref[0], _TN, preferred_element_type=F32)
                            + lax.dot_general(at_ref[b, 1], bsim_ref[1], _TN, preferred_element_type=F32))
    a_re = jnp.broadcast_to(a16_ref[0, 0:1, :] + a16_ref[1, 0:1, :], (nb, LANES))
    a_im = jnp.broadcast_to(a16_ref[0, 1:2, :] + a16_ref[1, 1:2, :], (nb, LANES))

    def step(j, carry):
        xr, xi = carry
        idx = pl.ds(j, nb, stride=nj)
        xre_ref[idx, :] = xr
        xim_ref[idx, :] = xi
        zr = zre_ref[idx, :]
        zi = zim_ref[idx, :]
        return a_re * xr - a_im * xi + zr, a_re * xi + a_im * xr + zi

    for b in range(nb):
        for i in range(2):
            yloc_ref[b, i] = jnp.dot(mt_ref[i], at_ref[b, i], preferred_element_type=F32)
    carry = (jnp.zeros((nb, LANES), F32), jnp.zeros((nb, LANES), F32))
    for j in range(nj):
        carry = step(j, carry)
    xcat = [jnp.concatenate([xre_ref[pl.ds(b * nj, nj), :], xim_ref[pl.ds(b * nj, nj), :]], axis=1).astype(BF16)
            for b in range(nb)]
    cross = [[lax.dot_general(ccat_ref[i], xcat[b], _NT, preferred_element_type=F32) for i in range(2)]
             for b in range(nb)]
    for b in range(nb):
        for i in range(2):
            yt_ref[b, i] = (yloc_ref[b, i] + cross[b][i]).astype(yt_ref.dtype)


def _s5_core(at, mt, bsre, bsim, ccat, a16):
    nb, G, _, nj = at.shape
    pair = lambda shape: pl.BlockSpec((2,) + shape, lambda p: (p, 0, 0))
    data = pl.BlockSpec((nb, 2, S5_BLK, nj), lambda p: (0, p, 0, 0))
    return pl.pallas_call(
        functools.partial(_s5_core_kernel, nb=nb, nj=nj),
        grid=(G // 2,),
        in_specs=[data, pair((S5_BLK, S5_BLK)), pair((S5_BLK, LANES)),
                  pair((S5_BLK, LANES)), pair((S5_BLK, 2 * LANES)), pair((8, LANES))],
        out_specs=data,
        out_shape=jax.ShapeDtypeStruct(at.shape, BF16),
        scratch_shapes=[pltpu.VMEM((nb * nj, LANES), F32)] * 4 + [pltpu.VMEM((nb, 2, S5_BLK, nj), F32)],
        compiler_params=_cparams(("parallel",)),
        name="s5_core",
    )(at, mt, bsre, bsim, ccat, a16)


def _ret_kernel(q_ref, k_ref, v_ref, cos_ref, sin_ref, dm_ref, qdec_ref, kdec_ref, cdec_ref,
                o3_ref, s_ref, qs_ref, ks_ref, o_ref, *, n_blocks):
    @pl.when(pl.program_id(1) == 0)
    def _():
        s_ref[...] = jnp.zeros(s_ref.shape, s_ref.dtype)

    cos = cos_ref[...]
    sin = sin_ref[...]
    width = q_ref.shape[-1]
    lane = lax.broadcasted_iota(jnp.int32, cos.shape, 1)
    first = (lane % RET_DK) < (RET_DK // 2)

    def rope(x):
        partner = jnp.where(first, pltpu.roll(x, width - RET_DK // 2, 1), pltpu.roll(x, RET_DK // 2, 1))
        return x * cos + partner * sin

    qs_ref[...] = (rope(q_ref[...].astype(F32)) * (RET_DK ** -0.5)).astype(BF16)
    ks_ref[...] = rope(k_ref[...].astype(F32)).astype(BF16)

    lane_p = lax.broadcasted_iota(jnp.int32, (RET_BLK, LANES), 1)
    heads = range(RET_HEADS)
    for c in range(n_blocks):
        rows = pl.ds(c * RET_BLK, RET_BLK)
        qp = [qs_ref[rows, pl.ds((h // 2) * LANES, LANES)] for h in heads]
        km = [jnp.where((lane_p // RET_DK) == (h % 2), ks_ref[rows, pl.ds((h // 2) * LANES, LANES)],
                        jnp.zeros((RET_BLK, LANES), BF16)) for h in heads]
        vh = [v_ref[rows, pl.ds(h * RET_DV, RET_DV)] for h in heads]
        sc = [lax.dot_general(qp[h], km[h], _NT, preferred_element_type=F32) for h in heads]
        s_prev = [s_ref[h] for h in heads]
        cross = [jnp.dot(qp[h], s_prev[h].astype(BF16), preferred_element_type=F32) for h in heads]
        p = [(sc[h] * dm_ref[h]).astype(BF16) for h in heads]
        o = [jnp.dot(p[h], vh[h], preferred_element_type=F32) + qdec_ref[h] * cross[h] for h in heads]
        vd = [(vh[h].astype(F32) * kdec_ref[h]).astype(BF16) for h in heads]
        kv = [lax.dot_general(km[h], vd[h], _TN, preferred_element_type=F32) for h in heads]
        for h in heads:
            s_ref[h] = cdec_ref[h] * s_prev[h] + kv[h]
        mu = [jnp.mean(o[h], axis=-1, keepdims=True) for h in heads]
        var = [jnp.mean(jnp.square(o[h] - mu[h]), axis=-1, keepdims=True) for h in heads]
        for h in heads:
            res = (o[h] - mu[h]) * lax.rsqrt(var[h] + EPS)
            for j in range(RET_BLK // S5_R):
                j0 = (c * (RET_BLK // S5_R) + j) * FRAME_PITCH
                o_ref[h, j0:j0 + S5_R, :] = res[j * S5_R:(j + 1) * S5_R]
    jt = o3_ref.shape[2]
    for s in range(S5_R):
        o3_ref[0, s] = jnp.concatenate(
            [o_ref[h, pl.ds(s, jt, stride=FRAME_PITCH), :] for h in range(RET_HEADS)], axis=1).astype(o3_ref.dtype)


def _retention(q, k, v, cos, sin, dm, qdec, kdec, cdec, nb, seq, tb):
    assert tb % RET_BLK == 0
    nt = seq // tb
    row_blk = lambda wd: pl.BlockSpec((tb, wd), lambda b, t: (b * nt + t, 0))
    pos_blk = lambda wd: pl.BlockSpec((tb, wd), lambda b, t: (t, 0))
    full = lambda a: pl.BlockSpec(a.shape, lambda b, t: (0,) * a.ndim)
    qk_w, v_w = q.shape[-1], v.shape[-1]
    return pl.pallas_call(
        functools.partial(_ret_kernel, n_blocks=tb // RET_BLK),
        grid=(nb, nt),
        in_specs=[row_blk(qk_w), row_blk(qk_w), row_blk(v_w), pos_blk(qk_w), pos_blk(qk_w),
                  full(dm), full(qdec), full(kdec), full(cdec)],
        out_specs=pl.BlockSpec((1, S5_R, tb // S5_R, v_w), lambda b, t: (b, 0, t, 0)),
        out_shape=jax.ShapeDtypeStruct((nb, S5_R, seq // S5_R, v_w), BF16),
        scratch_shapes=[pltpu.VMEM((RET_HEADS, LANES, RET_DV), F32),
                        pltpu.VMEM((tb, qk_w), BF16), pltpu.VMEM((tb, qk_w), BF16),
                        _frame_scratch(v_w, tb)],
        compiler_params=_cparams(("parallel", "arbitrary")),
        name="retention",
    )(q, k, v, cos, sin, dm, qdec, kdec, cdec)


MERGE_CB = 256
def _merge0_kernel(x_ref, y5t_ref, g5t_ref, yr_ref, gr_ref, gwt_ref, gb_ref, wo5_ref, wor_ref, o_ref):
    nj = x_ref.shape[1]
    s5_w = gwt_ref.shape[0]
    y = jax.nn.gelu(y5t_ref[0].astype(F32).reshape(-1, nj))
    yb = y.astype(BF16)
    r = yr_ref[0].astype(F32) * jax.nn.silu(gr_ref[0].astype(F32))
    blocks = [pl.ds(c0, MERGE_CB) for c0 in range(0, s5_w, MERGE_CB)]
    gates = [jnp.dot(gwt_ref[cb, :], yb, preferred_element_type=F32) + gb_ref[cb, :] for cb in blocks]
    out = x_ref[0] + jnp.dot(r.astype(BF16), wor_ref[...], preferred_element_type=F32)
    for n, cb in enumerate(blocks):
        a = (y[n * MERGE_CB:(n + 1) * MERGE_CB] * jax.nn.sigmoid(gates[n])
             * jax.nn.silu(g5t_ref[0, cb, :].astype(F32)))
        out = out + lax.dot_general(a.astype(BF16), wo5_ref[cb, :], _TN, preferred_element_type=F32)
    o_ref[0] = out


def _merge0(x3, y5t, g5t, yr3, gr3, gwt, gb_col, wo5, wor):
    nb, _, nj, d = x3.shape
    G = y5t.shape[1]
    s5_w = g5t.shape[2]
    ret_w = wor.shape[0]
    y5t5 = y5t.reshape(nb, G, S5_R, S5_GROUP, nj)
    strided = lambda wd: pl.BlockSpec((1, None, nj, wd), lambda b, s: (b, s, 0, 0))
    full = lambda a: pl.BlockSpec(a.shape, lambda b, s: (0,) * a.ndim)
    return pl.pallas_call(
        _merge0_kernel,
        grid=(nb, S5_R),
        in_specs=[strided(d),
                  pl.BlockSpec((1, G, None, S5_GROUP, nj), lambda b, s: (b, 0, s, 0, 0)),
                  pl.BlockSpec((1, None, s5_w, nj), lambda b, s: (b, s, 0, 0)),
                  strided(ret_w), strided(ret_w), full(gwt), full(gb_col), full(wo5), full(wor)],
        out_specs=strided(d),
        out_shape=jax.ShapeDtypeStruct(x3.shape, F32),
        compiler_params=_cparams(("parallel", "parallel")),
        name="merge0",
    )(x3, y5t5, g5t, yr3, gr3, gwt, gb_col, wo5, wor)


CONV_CARRY = 8
CONV_TB = 8
MM_COLS = 256
LOOP_TRIPS = 4
assert (CONV_K - 1 + S5_R - 1) // S5_R <= CONV_CARRY


def _layer1_kernel(xa_ref, xc_ref, vec_ref, win_ref, cw_ref, wout_ref, o_ref,
                   ubuf0_ref, ubuf1_ref, gate0_ref, gate1_ref, sh_ref, acc_ref, scr_ref,
                   hn_ref, pr_ref, u_ref, od_ref, *, jt, width, nt):
    i = pl.program_id(0)
    rows = S5_R * jt
    d = xa_ref.shape[-1]
    n_shift = sh_ref.shape[1]
    n_strips = acc_ref.shape[0]
    n_proj, n_out = win_ref.shape[0], wout_ref.shape[0]
    n_ab = 2 * width // MM_COLS
    assert n_strips % LOOP_TRIPS == 0 and n_proj % LOOP_TRIPS == 0 and n_out == LOOP_TRIPS

    @pl.when(i == 0)
    def _():
        ubuf1_ref[...] = jnp.zeros(ubuf1_ref.shape, F32)
        gate1_ref[...] = jnp.zeros(gate1_ref.shape, F32)
        u_ref[...] = jnp.zeros(u_ref.shape, u_ref.dtype)

    def conv_strip(ubuf_b, c):
        for dl in range(1, n_shift + 1):
            sh_ref[c, dl - 1] = ubuf_b[c, :, CONV_CARRY - dl:CONV_CARRY - dl + jt, :]
        for s0 in range(0, S5_R, CONV_TB):
            targets = range(s0, s0 + CONV_TB)
            accs = {s: jnp.broadcast_to(cw_ref[c, CONV_K:CONV_K + 1, :], (jt, LANES)) for s in targets}
            for e in range(s0 - (CONV_K - 1), s0 + CONV_TB):
                plane, dl = e % S5_R, -(e // S5_R)
                src = ubuf_b[c, plane, CONV_CARRY:CONV_CARRY + jt, :] if dl == 0 else sh_ref[c, dl - 1, plane]
                for s in targets:
                    kk = e + (CONV_K - 1) - s
                    if 0 <= kk < CONV_K:
                        accs[s] = accs[s] + src * cw_ref[c, kk:kk + 1, :]
            for s in targets:
                acc_ref[c, s * jt:(s + 1) * jt, :] = accs[s]

    def step(ubuf_a, gate_a, ubuf_b, gate_b):
        hn_ref[...] = _rms(xa_ref[0].reshape(rows, d), vec_ref[0:1, :]).astype(BF16)

        def body(it, carry):
            for k in range(n_strips // LOOP_TRIPS):
                conv_strip(ubuf_b, it * (n_strips // LOOP_TRIPS) + k)
            for k in range(n_proj // LOOP_TRIPS):
                n = it * (n_proj // LOOP_TRIPS) + k
                pr_ref[n] = jnp.dot(hn_ref[...], win_ref[n], preferred_element_type=F32)
            od_ref[it] = jnp.dot(u_ref[...], wout_ref[it], preferred_element_type=F32)
            return carry

        lax.fori_loop(0, LOOP_TRIPS, body, 0)

        x1 = xc_ref[0].reshape(rows, d) + jnp.concatenate([od_ref[n] for n in range(n_out)], axis=1)
        _from_frames(_rms(x1, vec_ref[3:4, :]), scr_ref, o_ref)

        acc = jnp.concatenate([acc_ref[c] for c in range(n_strips)], axis=1)
        mu = jnp.mean(acc, axis=-1, keepdims=True)
        var = jnp.mean(jnp.square(acc - mu), axis=-1, keepdims=True)
        uf = (acc - mu) * lax.rsqrt(var + EPS) * vec_ref[1:2, :] + vec_ref[2:3, :]
        gate = jnp.concatenate([gate_b[n] for n in range(gate_b.shape[0])], axis=1)
        u_ref[...] = (jax.nn.silu(uf) * jax.nn.silu(gate)).astype(u_ref.dtype)

        first = (i % nt) == 0
        for n in range(n_proj - n_ab):
            gate_a[n] = pr_ref[n_ab + n]
        for n in range(n_ab // 2):
            glu = pr_ref[n] * jax.nn.sigmoid(pr_ref[n_ab // 2 + n])
            for k in range(MM_COLS // LANES):
                c = n * (MM_COLS // LANES) + k
                ubuf_a[c, :, CONV_CARRY:CONV_CARRY + jt, :] = glu[:, k * LANES:(k + 1) * LANES].reshape(S5_R, jt, LANES)
        for c in range(n_strips):
            hist = ubuf_b[c, :, jt:jt + CONV_CARRY, :]
            ubuf_a[c, :, 0:CONV_CARRY, :] = jnp.where(first, jnp.zeros_like(hist), hist)

    @pl.when(i % 2 == 0)
    def _():
        step(ubuf0_ref, gate0_ref, ubuf1_ref, gate1_ref)

    @pl.when(i % 2 == 1)
    def _():
        step(ubuf1_ref, gate1_ref, ubuf0_ref, gate0_ref)


def _layer1(x3, g, win, cw, cb, lg, lb, wout, fg, tm):
    nb, _, nj, d = x3.shape
    width = wout.shape[0]
    jt = tm // S5_R
    nt = nj // jt
    n_tiles = nb * nt
    n_strips = width // LANES
    n_shift = (CONV_K - 1 + S5_R - 1) // S5_R
    win3 = win.reshape(d, 3 * width // MM_COLS, MM_COLS).transpose(1, 0, 2)
    wout3 = wout.reshape(width, d // MM_COLS, MM_COLS).transpose(1, 0, 2)
    cw3 = jnp.concatenate([cw, cb], axis=0).reshape(CONV_K + 1, n_strips, LANES).transpose(1, 0, 2)
    vec = jnp.concatenate([g, lg, lb, fg, jnp.zeros((4, d), F32)], axis=0)
    full = lambda a: pl.BlockSpec(a.shape, lambda i: (0,) * a.ndim, pipeline_mode=pl.Buffered(1))
    tile_a = lambda i: jnp.minimum(i, n_tiles - 1)
    tile_c = lambda i: jnp.clip(i - 2, 0, n_tiles - 1)
    plane_blk = lambda tile: pl.BlockSpec((1, S5_R, jt, d), lambda i: (tile(i) // nt, 0, tile(i) % nt, 0))
    ubuf = pltpu.VMEM((n_strips, S5_R, CONV_CARRY + jt, LANES), F32)
    gate = pltpu.VMEM((width // MM_COLS, tm, MM_COLS), F32)
    return pl.pallas_call(
        functools.partial(_layer1_kernel, jt=jt, width=width, nt=nt),
        grid=(n_tiles + 2,),
        in_specs=[plane_blk(tile_a), plane_blk(tile_c),
                  full(vec), full(win3), full(cw3), full(wout3)],
        out_specs=pl.BlockSpec((tm, d), lambda i: (tile_c(i), 0)),
        out_shape=jax.ShapeDtypeStruct((nb * nj * S5_R, d), F32),
        scratch_shapes=[ubuf, ubuf, gate, gate,
                        pltpu.VMEM((n_strips, n_shift, S5_R, jt, LANES), F32),
                        pltpu.VMEM((n_strips, tm, LANES), F32),
                        _frame_scratch(d, tm),
                        pltpu.VMEM((tm, d), BF16),
                        pltpu.VMEM((3 * width // MM_COLS, tm, MM_COLS), F32),
                        pltpu.VMEM((tm, width), BF16),
                        pltpu.VMEM((d // MM_COLS, tm, MM_COLS), F32)],
        compiler_params=_cparams(("arbitrary",)),
        name="layer1",
    )(x3, x3, vec, win3, cw3, wout3)


def _pick_tile(n, pref):
    t = min(n, pref)
    while n % t:
        t //= 2
    return t


def kernel(x, norm_g, final_g, w_in_ab, s5_a_re, s5_a_im, s5_log_dt, s5_b_re, s5_b_im, s5_c_re, s5_c_im,
           s5_d, s5_glu_w, s5_glu_b, w_out_ab, w_in_c, conv_w, conv_b, conv_ln_g, conv_ln_b, w_out_c):
    nb, seq, d = x.shape
    rows = nb * seq
    s5_w = s5_glu_w.shape[-1]
    G = s5_w // S5_GROUP
    ret_qk = RET_HEADS * RET_DK
    ret_w = RET_HEADS * RET_DV
    nj = seq // S5_R
    x2 = x.reshape(rows, d)

    w_in = w_in_ab[0].astype(BF16)
    tm = _pick_tile(seq, 512)
    q, k, v, gr3, hn3, x3 = _proj0(x2, norm_g[0:1], w_in[:, 2 * s5_w:], nb, seq, ret_qk, ret_w, tm)
    at, g5t = _proj_u(hn3, w_in[:, :2 * s5_w].T, s5_w, 4)

    dup = lambda a: jnp.concatenate([a, a], axis=-1)
    are2 = dup(s5_a_re[0])[:, None, :]
    aim2 = dup(s5_a_im[0])[:, None, :]
    ldt = s5_log_dt[0][:, None, None]
    bre2 = dup(jnp.swapaxes(s5_b_re[0], 1, 2))
    bim2 = dup(jnp.swapaxes(s5_b_im[0], 1, 2))
    cre2 = dup(s5_c_re[0])
    cim2 = dup(s5_c_im[0])
    dsk_t = jnp.tile(s5_d[0].reshape(G, 1, S5_GROUP), (1, S5_R, 1)).reshape(G, S5_BLK, 1)
    mt, bsre, bsim, ccat, a16 = _s5_prep(are2, aim2, ldt, bre2, bim2, cre2, cim2, dsk_t)

    y5t = _s5_core(at, mt, bsre, bsim, ccat, a16)

    pos = jnp.arange(seq, dtype=F32)[:, None]
    half = RET_DK // 2
    freqs = ROPE_BASE ** (-jnp.arange(half, dtype=F32) / half)
    ang = pos * freqs[None, :]
    cos_h = jnp.concatenate([jnp.cos(ang), jnp.cos(ang)], axis=-1)
    sin_h = jnp.concatenate([-jnp.sin(ang), jnp.sin(ang)], axis=-1)
    cos_t = jnp.tile(cos_h, (1, RET_HEADS))
    sin_t = jnp.tile(sin_h, (1, RET_HEADS))
    log_g = jnp.log(1.0 - 2.0 ** (-5.0 - jnp.arange(RET_HEADS, dtype=F32)))
    idx = jnp.arange(RET_BLK, dtype=F32)
    chunk_of = jnp.arange(RET_BLK) // CHUNK
    seen = chunk_of[:, None] >= chunk_of[None, :]
    dm = jnp.where(seen[None], jnp.exp(log_g[:, None, None] * jnp.abs(idx[:, None] - idx[None, :])[None]), 0.0)
    qdec = jnp.exp(log_g[:, None] * (idx + 1.0)[None, :])[:, :, None]
    kdec = jnp.exp(log_g[:, None] * (RET_BLK - 1 - idx)[None, :])[:, :, None]
    cdec = jnp.broadcast_to(jnp.exp(log_g * RET_BLK)[:, None, None], (RET_HEADS, 1, RET_DV))
    yr3 = _retention(q, k, v, cos_t, sin_t, dm, qdec, kdec, cdec, nb, seq, tm)

    wo = w_out_ab[0].astype(BF16)
    x3 = _merge0(x3, y5t, g5t, yr3, gr3, s5_glu_w[0].T.astype(BF16), s5_glu_b[0][:, None],
                 wo[:s5_w], wo[s5_w:])

    out = _layer1(x3, norm_g[1:2], w_in_c[0].astype(BF16), conv_w[0], conv_b[0:1], conv_ln_g[0:1],
                  conv_ln_b[0:1], w_out_c[0].astype(BF16), final_g[None, :], tm)
    return out.reshape(nb, seq, d)
```

```python
import functools
import math

import jax
import jax.numpy as jnp
from jax import lax
from jax.experimental import pallas as pl
from jax.experimental.pallas import tpu as pltpu

F32 = jnp.float32
BF16 = jnp.bfloat16

EPS = 1e-6
CHUNK = 64
S5_GROUP = 16
S5_STATE = 64
S5_R = 16
S5_BLK = S5_R * S5_GROUP
RET_HEADS = 8
RET_DK = 64
RET_DV = 128
RET_BLK = 256
ROPE_BASE = 10000.0
CONV_K = 31
CONV_HALO = 32

LANES = 128
VMEM_LIMIT_BYTES = 56 * 1024 * 1024

_NT = (((1,), (1,)), ((), ()))
_TN = (((0,), (0,)), ((), ()))


def _cparams(sem):
    return pltpu.CompilerParams(dimension_semantics=sem, vmem_limit_bytes=VMEM_LIMIT_BYTES)


def _rms(x, g):
    return x * lax.rsqrt(jnp.mean(x * x, axis=-1, keepdims=True) + EPS) * g


FRAME_PITCH = 24
assert FRAME_PITCH >= S5_R and FRAME_PITCH % 8 == 0


def _frame_scratch(width, rows):
    return pltpu.VMEM((width // LANES, rows // S5_R * FRAME_PITCH, LANES), F32)


def _to_frames(val, scr_ref, dst_ref):
    nc = scr_ref.shape[0]
    jt = scr_ref.shape[1] // FRAME_PITCH
    for c in range(nc):
        for j in range(jt):
            scr_ref[c, j * FRAME_PITCH:j * FRAME_PITCH + S5_R, :] = val[j * S5_R:(j + 1) * S5_R, c * LANES:(c + 1) * LANES]
    for s in range(S5_R):
        dst_ref[0, s] = jnp.concatenate(
            [scr_ref[c, pl.ds(s, jt, stride=FRAME_PITCH), :] for c in range(nc)], axis=1).astype(dst_ref.dtype)


def _from_frames(val, scr_ref, dst_ref):
    nc = scr_ref.shape[0]
    jt = scr_ref.shape[1] // FRAME_PITCH
    for s in range(S5_R):
        for c in range(nc):
            scr_ref[c, pl.ds(s, jt, stride=FRAME_PITCH), :] = val[s * jt:(s + 1) * jt, c * LANES:(c + 1) * LANES]
    for j in range(jt):
        dst_ref[j * S5_R:(j + 1) * S5_R, :] = jnp.concatenate(
            [scr_ref[c, j * FRAME_PITCH:j * FRAME_PITCH + S5_R, :] for c in range(nc)], axis=1)


def _proj0_kernel(x_ref, g_ref, w_ref, q_ref, k_ref, v_ref, gr3_ref, hn3_ref, x3_ref, scr_ref):
    x = x_ref[...]
    hn32 = _rms(x, g_ref[...])
    hn = hn32.astype(BF16)
    off = 0
    for ref in (q_ref, k_ref, v_ref):
        wd = ref.shape[-1]
        ref[...] = jnp.dot(hn, w_ref[:, off:off + wd], preferred_element_type=F32).astype(ref.dtype)
        off += wd
    _to_frames(hn32, scr_ref, hn3_ref)
    _to_frames(x, scr_ref, x3_ref)
    _to_frames(jnp.dot(hn, w_ref[:, off:], preferred_element_type=F32), scr_ref, gr3_ref)


def _proj0(x2, g, w, nb, seq, qk_w, v_w, tm):
    rows, d = x2.shape
    nt = seq // tm
    jt = tm // S5_R
    nj = seq // S5_R
    nat = lambda wd: pl.BlockSpec((tm, wd), lambda b, t: (b * nt + t, 0))
    fm = lambda wd: pl.BlockSpec((1, S5_R, jt, wd), lambda b, t: (b, 0, t, 0))
    fm_shape = lambda wd, dt: jax.ShapeDtypeStruct((nb, S5_R, nj, wd), dt)
    assert d == v_w, "the strided-read scratch is shared between hn and the retention gate"
    return pl.pallas_call(
        _proj0_kernel,
        grid=(nb, nt),
        in_specs=[nat(d), pl.BlockSpec((1, d), lambda b, t: (0, 0)), pl.BlockSpec(w.shape, lambda b, t: (0, 0))],
        out_specs=[nat(qk_w), nat(qk_w), nat(v_w), fm(v_w), fm(d), fm(d)],
        out_shape=[jax.ShapeDtypeStruct((rows, qk_w), BF16), jax.ShapeDtypeStruct((rows, qk_w), BF16),
                   jax.ShapeDtypeStruct((rows, v_w), BF16), fm_shape(v_w, BF16), fm_shape(d, BF16),
                   fm_shape(d, F32)],
        scratch_shapes=[_frame_scratch(d, tm)],
        compiler_params=_cparams(("parallel", "parallel")),
        name="proj0",
    )(x2, g, w)


def _proj_u_kernel(hn3_ref, wt_ref, at_ref, g5t_ref, *, sg, s5_w):
    nj = hn3_ref.shape[2]
    for sl in range(sg):
        ut = lax.dot_general(wt_ref[...], hn3_ref[0, sl], _NT, preferred_element_type=F32)
        at_ref[0, :, sl * S5_GROUP:(sl + 1) * S5_GROUP, :] = (
            ut[:s5_w].reshape(s5_w // S5_GROUP, S5_GROUP, nj).astype(at_ref.dtype))
        g5t_ref[0, sl] = ut[s5_w:].astype(g5t_ref.dtype)


def _proj_u(hn3, wt, s5_w, sg):
    nb, _, nj, d = hn3.shape
    G = s5_w // S5_GROUP
    return pl.pallas_call(
        functools.partial(_proj_u_kernel, sg=sg, s5_w=s5_w),
        grid=(nb, S5_R // sg),
        in_specs=[
            pl.BlockSpec((1, sg, nj, d), lambda b, s: (b, s, 0, 0)),
            pl.BlockSpec(wt.shape, lambda b, s: (0, 0)),
        ],
        out_specs=[
            pl.BlockSpec((1, G, sg * S5_GROUP, nj), lambda b, s: (b, 0, s, 0)),
            pl.BlockSpec((1, sg, s5_w, nj), lambda b, s: (b, s, 0, 0)),
        ],
        out_shape=[jax.ShapeDtypeStruct((nb, G, S5_BLK, nj), BF16),
                   jax.ShapeDtypeStruct((nb, S5_R, s5_w, nj), BF16)],
        compiler_params=_cparams(("parallel", "parallel")),
        name="proj_u",
    )(hn3, wt)


def _s5_prep_kernel(are_ref, aim_ref, ldt_ref, bre_ref, bim_ref, cre_ref, cim_ref, dsk_ref,
                    mt_ref, bsre_ref, bsim_ref, ccat_ref, a16_ref):
    g = pl.program_id(0)
    are = are_ref[0]
    aim = aim_ref[0]
    dt = jnp.exp(ldt_ref[0])
    bre, bim = bre_ref[0], bim_ref[0]
    cre, cim = cre_ref[0], cim_ref[0]

    def lpow(e):
        mag = jnp.exp(are * dt * e)
        ang = aim * dt * e
        return mag * jnp.cos(ang), mag * jnp.sin(ang)

    one = jnp.ones((1, 1), F32)
    lb_re, lb_im = lpow(one)
    nr, ni = lb_re - 1.0, lb_im
    den = are * are + aim * aim
    coef_re = (nr * are + ni * aim) / den
    coef_im = (ni * are - nr * aim) / den
    bb_re = coef_re * bre - coef_im * bim
    bb_im = coef_re * bim + coef_im * bre

    p = lax.broadcasted_iota(jnp.int32, (S5_R, 1), 0).astype(F32)
    lane = lax.broadcasted_iota(jnp.int32, (1, LANES), 1)
    mine = (lane // S5_STATE) == (g % 2)
    low = lane < S5_STATE

    def outer(t_re, t_im, m_re, m_im):
        re = t_re[:, None, :] * m_re[None, :, :] - t_im[:, None, :] * m_im[None, :, :]
        im = t_re[:, None, :] * m_im[None, :, :] + t_im[:, None, :] * m_re[None, :, :]
        return re.reshape(S5_BLK, LANES), im.reshape(S5_BLK, LANES)

    half = float(S5_R // 2)
    e_re, e_im = lpow(p - half)
    f_re, f_im = lpow(half - p)
    p_re, p_im = outer(e_re, e_im, cre, cim)
    q_re, q_im = outer(f_re, f_im, bb_re, bb_im)
    q_re = jnp.where(low, q_re, 0.0)
    q_im = jnp.where(low, q_im, 0.0)
    mt = (lax.dot_general(p_re, q_re, _NT, precision=lax.Precision.HIGHEST, preferred_element_type=F32)
          - lax.dot_general(p_im, q_im, _NT, precision=lax.Precision.HIGHEST, preferred_element_type=F32))
    row = lax.broadcasted_iota(jnp.int32, (S5_BLK, S5_BLK), 0)
    col = lax.broadcasted_iota(jnp.int32, (S5_BLK, S5_BLK), 1)
    mt = jnp.where(row // S5_GROUP >= col // S5_GROUP, mt, 0.0)
    mt = mt + jnp.where(row == col, dsk_ref[0], 0.0)
    mt_ref[0] = mt.astype(mt_ref.dtype)

    g_re, g_im = lpow(float(S5_R - 1) - p)
    bs_re, bs_im = outer(g_re, g_im, bb_re, bb_im)
    bsre_ref[0] = jnp.where(mine, bs_re, 0.0).astype(bsre_ref.dtype)
    bsim_ref[0] = jnp.where(mine, bs_im, 0.0).astype(bsim_ref.dtype)

    h_re, h_im = lpow(p + 1.0)
    cs_re, cs_im = outer(h_re, h_im, cre, cim)
    ccat_ref[0, :, 0:LANES] = jnp.where(mine, cs_re, 0.0).astype(ccat_ref.dtype)
    ccat_ref[0, :, LANES:2 * LANES] = jnp.where(mine, -cs_im, 0.0).astype(ccat_ref.dtype)

    a_re, a_im = lpow(one * float(S5_R))
    rows8 = lax.broadcasted_iota(jnp.int32, (8, LANES), 0)
    a16_ref[0] = jnp.where(mine, jnp.where(rows8 == 0, a_re, jnp.where(rows8 == 1, a_im, 0.0)), 0.0)


def _s5_prep(are2, aim2, ldt, bre2, bim2, cre2, cim2, dsk_t):
    G = are2.shape[0]
    spec3 = lambda shape: pl.BlockSpec((1,) + shape, lambda g: (g, 0, 0))
    return pl.pallas_call(
        _s5_prep_kernel,
        grid=(G,),
        in_specs=[spec3((1, LANES)), spec3((1, LANES)), spec3((1, 1)),
                  spec3((S5_GROUP, LANES)), spec3((S5_GROUP, LANES)),
                  spec3((S5_GROUP, LANES)), spec3((S5_GROUP, LANES)), spec3((S5_BLK, 1))],
        out_specs=[spec3((S5_BLK, S5_BLK)), spec3((S5_BLK, LANES)), spec3((S5_BLK, LANES)),
                   spec3((S5_BLK, 2 * LANES)), spec3((8, LANES))],
        out_shape=[jax.ShapeDtypeStruct((G, S5_BLK, S5_BLK), BF16),
                   jax.ShapeDtypeStruct((G, S5_BLK, LANES), BF16),
                   jax.ShapeDtypeStruct((G, S5_BLK, LANES), BF16),
                   jax.ShapeDtypeStruct((G, S5_BLK, 2 * LANES), BF16),
                   jax.ShapeDtypeStruct((G, 8, LANES), F32)],
        compiler_params=_cparams(("parallel",)),
        name="s5_prep",
    )(are2, aim2, ldt, bre2, bim2, cre2, cim2, dsk_t)


def _s5_core_kernel(at_ref, mt_ref, bsre_ref, bsim_ref, ccat_ref, a16_ref, yt_ref,
                    zre_ref, zim_ref, xre_ref, xim_ref, yloc_ref, *, nb, nj):
    for b in range(nb):
        rows = pl.ds(b * nj, nj)
        zre_ref[rows, :] = (lax.dot_general(at_ref[b, 0], bsre_ref[0], _TN, preferred_element_type=F32)
                            + lax.dot_general(at_ref[b, 1], bsre_ref[1], _TN, preferred_element_type=F32))
        zim_ref[rows, :] = (lax.dot_general(at_ref[b, 0], bsim_ref[0], _TN, preferred_element_type=F32)
                            + lax.dot_general(at_ref[b, 1], bsim_ref[1], _TN, preferred_element_type=F32))
    a_re = jnp.broadcast_to(a16_ref[0, 0:1, :] + a16_ref[1, 0:1, :], (nb, LANES))
    a_im = jnp.broadcast_to(a16_ref[0, 1:2, :] + a16_ref[1, 1:2, :], (nb, LANES))

    def step(j, carry):
        xr, xi = carry
        idx = pl.ds(j, nb, stride=nj)
        xre_ref[idx, :] = xr
        xim_ref[idx, :] = xi
        zr = zre_ref[idx, :]
        zi = zim_ref[idx, :]
        return a_re * xr - a_im * xi + zr, a_re * xi + a_im * xr + zi

    for b in range(nb):
        for i in range(2):
            yloc_ref[b, i] = jnp.dot(mt_ref[i], at_ref[b, i], preferred_element_type=F32)
    carry = (jnp.zeros((nb, LANES), F32), jnp.zeros((nb, LANES), F32))
    for j in range(nj):
        carry = step(j, carry)
    xcat = [jnp.concatenate([xre_ref[pl.ds(b * nj, nj), :], xim_ref[pl.ds(b * nj, nj), :]], axis=1).astype(BF16)
            for b in range(nb)]
    cross = [[lax.dot_general(ccat_ref[i], xcat[b], _NT, preferred_element_type=F32) for i in range(2)]
             for b in range(nb)]
    for b in range(nb):
        for i in range(2):
            yt_ref[b, i] = (yloc_ref[b, i] + cross[b][i]).astype(yt_ref.dtype)


def _s5_core(at, mt, bsre, bsim, ccat, a16):
    nb, G, _, nj = at.shape
    pair = lambda shape: pl.BlockSpec((2,) + shape, lambda p: (p, 0, 0))
    data = pl.BlockSpec((nb, 2, S5_BLK, nj), lambda p: (0, p, 0, 0))
    return pl.pallas_call(
        functools.partial(_s5_core_kernel, nb=nb, nj=nj),
        grid=(G // 2,),
        in_specs=[data, pair((S5_BLK, S5_BLK)), pair((S5_BLK, LANES)),
                  pair((S5_BLK, LANES)), pair((S5_BLK, 2 * LANES)), pair((8, LANES))],
        out_specs=data,
        out_shape=jax.ShapeDtypeStruct(at.shape, BF16),
        scratch_shapes=[pltpu.VMEM((nb * nj, LANES), F32)] * 4 + [pltpu.VMEM((nb, 2, S5_BLK, nj), F32)],
        compiler_params=_cparams(("parallel",)),
        name="s5_core",
    )(at, mt, bsre, bsim, ccat, a16)


def _ret_kernel(q_ref, k_ref, v_ref, cos_ref, sin_ref, dm_ref, qdec_ref, kdec_ref, cdec_ref,
                o3_ref, s_ref, qs_ref, ks_ref, o_ref, *, n_blocks):
    @pl.when(pl.program_id(1) == 0)
    def _():
        s_ref[...] = jnp.zeros(s_ref.shape, s_ref.dtype)

    cos = cos_ref[...]
    sin = sin_ref[...]
    width = q_ref.shape[-1]
    lane = lax.broadcasted_iota(jnp.int32, cos.shape, 1)
    first = (lane % RET_DK) < (RET_DK // 2)

    def rope(x):
        partner = jnp.where(first, pltpu.roll(x, width - RET_DK // 2, 1), pltpu.roll(x, RET_DK // 2, 1))
        return x * cos + partner * sin

    qs_ref[...] = (rope(q_ref[...].astype(F32)) * (RET_DK ** -0.5)).astype(BF16)
    ks_ref[...] = rope(k_ref[...].astype(F32)).astype(BF16)

    lane_p = lax.broadcasted_iota(jnp.int32, (RET_BLK, LANES), 1)
    heads = range(RET_HEADS)
    for c in range(n_blocks):
        rows = pl.ds(c * RET_BLK, RET_BLK)
        qp = [qs_ref[rows, pl.ds((h // 2) * LANES, LANES)] for h in heads]
        km = [jnp.where((lane_p // RET_DK) == (h % 2), ks_ref[rows, pl.ds((h // 2) * LANES, LANES)],
                        jnp.zeros((RET_BLK, LANES), BF16)) for h in heads]
        vh = [v_ref[rows, pl.ds(h * RET_DV, RET_DV)] for h in heads]
        sc = [lax.dot_general(qp[h], km[h], _NT, preferred_element_type=F32) for h in heads]
        s_prev = [s_ref[h] for h in heads]
        cross = [jnp.dot(qp[h], s_prev[h].astype(BF16), preferred_element_type=F32) for h in heads]
        p = [(sc[h] * dm_ref[h]).astype(BF16) for h in heads]
        o = [jnp.dot(p[h], vh[h], preferred_element_type=F32) + qdec_ref[h] * cross[h] for h in heads]
        vd = [(vh[h].astype(F32) * kdec_ref[h]).astype(BF16) for h in heads]
        kv = [lax.dot_general(km[h], vd[h], _TN, preferred_element_type=F32) for h in heads]
        for h in heads:
            s_ref[h] = cdec_ref[h] * s_prev[h] + kv[h]
        mu = [jnp.mean(o[h], axis=-1, keepdims=True) for h in heads]
        var = [jnp.mean(jnp.square(o[h] - mu[h]), axis=-1, keepdims=True) for h in heads]
        for h in heads:
            res = (o[h] - mu[h]) * lax.rsqrt(var[h] + EPS)
            for j in range(RET_BLK // S5_R):
                j0 = (c * (RET_BLK // S5_R) + j) * FRAME_PITCH
                o_ref[h, j0:j0 + S5_R, :] = res[j * S5_R:(j + 1) * S5_R]
    jt = o3_ref.shape[2]
    for s in range(S5_R):
        o3_ref[0, s] = jnp.concatenate(
            [o_ref[h, pl.ds(s, jt, stride=FRAME_PITCH), :] for h in range(RET_HEADS)], axis=1).astype(o3_ref.dtype)


def _retention(q, k, v, cos, sin, dm, qdec, kdec, cdec, nb, seq, tb):
    assert tb % RET_BLK == 0
    nt = seq // tb
    row_blk = lambda wd: pl.BlockSpec((tb, wd), lambda b, t: (b * nt + t, 0))
    pos_blk = lambda wd: pl.BlockSpec((tb, wd), lambda b, t: (t, 0))
    full = lambda a: pl.BlockSpec(a.shape, lambda b, t: (0,) * a.ndim)
    qk_w, v_w = q.shape[-1], v.shape[-1]
    return pl.pallas_call(
        functools.partial(_ret_kernel, n_blocks=tb // RET_BLK),
        grid=(nb, nt),
        in_specs=[row_blk(qk_w), row_blk(qk_w), row_blk(v_w), pos_blk(qk_w), pos_blk(qk_w),
                  full(dm), full(qdec), full(kdec), full(cdec)],
        out_specs=pl.BlockSpec((1, S5_R, tb // S5_R, v_w), lambda b, t: (b, 0, t, 0)),
        out_shape=jax.ShapeDtypeStruct((nb, S5_R, seq // S5_R, v_w), BF16),
        scratch_shapes=[pltpu.VMEM((RET_HEADS, LANES, RET_DV), F32),
                        pltpu.VMEM((tb, qk_w), BF16), pltpu.VMEM((tb, qk_w), BF16),
                        _frame_scratch(v_w, tb)],
        compiler_params=_cparams(("parallel", "arbitrary")),
        name="retention",
    )(q, k, v, cos, sin, dm, qdec, kdec, cdec)


MERGE_CB = 256
def _merge0_kernel(x_ref, y5t_ref, g5t_ref, yr_ref, gr_ref, gwt_ref, gb_ref, wo5_ref, wor_ref, o_ref):
    nj = x_ref.shape[1]
    s5_w = gwt_ref.shape[0]
    y = jax.nn.gelu(y5t_ref[0].astype(F32).reshape(-1, nj))
    yb = y.astype(BF16)
    r = yr_ref[0].astype(F32) * jax.nn.silu(gr_ref[0].astype(F32))
    blocks = [pl.ds(c0, MERGE_CB) for c0 in range(0, s5_w, MERGE_CB)]
    gates = [jnp.dot(gwt_ref[cb, :], yb, preferred_element_type=F32) + gb_ref[cb, :] for cb in blocks]
    out = x_ref[0] + jnp.dot(r.astype(BF16), wor_ref[...], preferred_element_type=F32)
    for n, cb in enumerate(blocks):
        a = (y[n * MERGE_CB:(n + 1) * MERGE_CB] * jax.nn.sigmoid(gates[n])
             * jax.nn.silu(g5t_ref[0, cb, :].astype(F32)))
        out = out + lax.dot_general(a.astype(BF16), wo5_ref[cb, :], _TN, preferred_element_type=F32)
    o_ref[0] = out


def _merge0(x3, y5t, g5t, yr3, gr3, gwt, gb_col, wo5, wor):
    nb, _, nj, d = x3.shape
    G = y5t.shape[1]
    s5_w = g5t.shape[2]
    ret_w = wor.shape[0]
    y5t5 = y5t.reshape(nb, G, S5_R, S5_GROUP, nj)
    strided = lambda wd: pl.BlockSpec((1, None, nj, wd), lambda b, s: (b, s, 0, 0))
    full = lambda a: pl.BlockSpec(a.shape, lambda b, s: (0,) * a.ndim)
    return pl.pallas_call(
        _merge0_kernel,
        grid=(nb, S5_R),
        in_specs=[strided(d),
                  pl.BlockSpec((1, G, None, S5_GROUP, nj), lambda b, s: (b, 0, s, 0, 0)),
                  pl.BlockSpec((1, None, s5_w, nj), lambda b, s: (b, s, 0, 0)),
                  strided(ret_w), strided(ret_w), full(gwt), full(gb_col), full(wo5), full(wor)],
        out_specs=strided(d),
        out_shape=jax.ShapeDtypeStruct(x3.shape, F32),
        compiler_params=_cparams(("parallel", "parallel")),
        name="merge0",
    )(x3, y5t5, g5t, yr3, gr3, gwt, gb_col, wo5, wor)


CONV_CARRY = 8
CONV_TB = 8
MM_COLS = 256
LOOP_TRIPS = 4
assert (CONV_K - 1 + S5_R - 1) // S5_R <= CONV_CARRY


def _layer1_kernel(xa_ref, xc_ref, vec_ref, win_ref, cw_ref, wout_ref, o_ref,
                   ubuf0_ref, ubuf1_ref, gate0_ref, gate1_ref, sh_ref, acc_ref, scr_ref,
                   hn_ref, pr_ref, u_ref, od_ref, *, jt, width, nt):
    i = pl.program_id(0)
    rows = S5_R * jt
    d = xa_ref.shape[-1]
    n_shift = sh_ref.shape[1]
    n_strips = acc_ref.shape[0]
    n_proj, n_out = win_ref.shape[0], wout_ref.shape[0]
    n_ab = 2 * width // MM_COLS
    assert n_strips % LOOP_TRIPS == 0 and n_proj % LOOP_TRIPS == 0 and n_out == LOOP_TRIPS

    @pl.when(i == 0)
    def _():
        ubuf1_ref[...] = jnp.zeros(ubuf1_ref.shape, F32)
        gate1_ref[...] = jnp.zeros(gate1_ref.shape, F32)
        u_ref[...] = jnp.zeros(u_ref.shape, u_ref.dtype)

    def conv_strip(ubuf_b, c):
        for dl in range(1, n_shift + 1):
            sh_ref[c, dl - 1] = ubuf_b[c, :, CONV_CARRY - dl:CONV_CARRY - dl + jt, :]
        for s0 in range(0, S5_R, CONV_TB):
            targets = range(s0, s0 + CONV_TB)
            accs = {s: jnp.broadcast_to(cw_ref[c, CONV_K:CONV_K + 1, :], (jt, LANES)) for s in targets}
            for e in range(s0 - (CONV_K - 1), s0 + CONV_TB):
                plane, dl = e % S5_R, -(e // S5_R)
                src = ubuf_b[c, plane, CONV_CARRY:CONV_CARRY + jt, :] if dl == 0 else sh_ref[c, dl - 1, plane]
                for s in targets:
                    kk = e + (CONV_K - 1) - s
                    if 0 <= kk < CONV_K:
                        accs[s] = accs[s] + src * cw_ref[c, kk:kk + 1, :]
            for s in targets:
                acc_ref[c, s * jt:(s + 1) * jt, :] = accs[s]

    def step(ubuf_a, gate_a, ubuf_b, gate_b):
        hn_ref[...] = _rms(xa_ref[0].reshape(rows, d), vec_ref[0:1, :]).astype(BF16)

        def body(it, carry):
            for k in range(n_strips // LOOP_TRIPS):
                conv_strip(ubuf_b, it * (n_strips // LOOP_TRIPS) + k)
            for k in range(n_proj // LOOP_TRIPS):
                n = it * (n_proj // LOOP_TRIPS) + k
                pr_ref[n] = jnp.dot(hn_ref[...], win_ref[n], preferred_element_type=F32)
            od_ref[it] = jnp.dot(u_ref[...], wout_ref[it], preferred_element_type=F32)
            return carry

        for it in range(LOOP_TRIPS):
            @pl.when(i >= 0)
            def _(it=it):
                body(it, 0)

        x1 = xc_ref[0].reshape(rows, d) + jnp.concatenate([od_ref[n] for n in range(n_out)], axis=1)
        _from_frames(_rms(x1, vec_ref[3:4, :]), scr_ref, o_ref)

        acc = jnp.concatenate([acc_ref[c] for c in range(n_strips)], axis=1)
        mu = jnp.mean(acc, axis=-1, keepdims=True)
        var = jnp.mean(jnp.square(acc - mu), axis=-1, keepdims=True)
        uf = (acc - mu) * lax.rsqrt(var + EPS) * vec_ref[1:2, :] + vec_ref[2:3, :]
        gate = jnp.concatenate([gate_b[n] for n in range(gate_b.shape[0])], axis=1)
        u_ref[...] = (jax.nn.silu(uf) * jax.nn.silu(gate)).astype(u_ref.dtype)

        first = (i % nt) == 0
        for n in range(n_proj - n_ab):
            gate_a[n] = pr_ref[n_ab + n]
        for n in range(n_ab // 2):
            glu = pr_ref[n] * jax.nn.sigmoid(pr_ref[n_ab // 2 + n])
            for k in range(MM_COLS // LANES):
                c = n * (MM_COLS // LANES) + k
                ubuf_a[c, :, CONV_CARRY:CONV_CARRY + jt, :] = glu[:, k * LANES:(k + 1) * LANES].reshape(S5_R, jt, LANES)
        for c in range(n_strips):
            hist = ubuf_b[c, :, jt:jt + CONV_CARRY, :]
            ubuf_a[c, :, 0:CONV_CARRY, :] = jnp.where(first, jnp.zeros_like(hist), hist)

    @pl.when(i % 2 == 0)
    def _():
        step(ubuf0_ref, gate0_ref, ubuf1_ref, gate1_ref)

    @pl.when(i % 2 == 1)
    def _():
        step(ubuf1_ref, gate1_ref, ubuf0_ref, gate0_ref)


def _layer1(x3, g, win, cw, cb, lg, lb, wout, fg, tm):
    nb, _, nj, d = x3.shape
    width = wout.shape[0]
    jt = tm // S5_R
    nt = nj // jt
    n_tiles = nb * nt
    n_strips = width // LANES
    n_shift = (CONV_K - 1 + S5_R - 1) // S5_R
    win3 = win.reshape(d, 3 * width // MM_COLS, MM_COLS).transpose(1, 0, 2)
    wout3 = wout.reshape(width, d // MM_COLS, MM_COLS).transpose(1, 0, 2)
    cw3 = jnp.concatenate([cw, cb], axis=0).reshape(CONV_K + 1, n_strips, LANES).transpose(1, 0, 2)
    vec = jnp.concatenate([g, lg, lb, fg, jnp.zeros((4, d), F32)], axis=0)
    full = lambda a: pl.BlockSpec(a.shape, lambda i: (0,) * a.ndim, pipeline_mode=pl.Buffered(1))
    tile_a = lambda i: jnp.minimum(i, n_tiles - 1)
    tile_c = lambda i: jnp.clip(i - 2, 0, n_tiles - 1)
    plane_blk = lambda tile: pl.BlockSpec((1, S5_R, jt, d), lambda i: (tile(i) // nt, 0, tile(i) % nt, 0))
    ubuf = pltpu.VMEM((n_strips, S5_R, CONV_CARRY + jt, LANES), F32)
    gate = pltpu.VMEM((width // MM_COLS, tm, MM_COLS), F32)
    return pl.pallas_call(
        functools.partial(_layer1_kernel, jt=jt, width=width, nt=nt),
        grid=(n_tiles + 2,),
        in_specs=[plane_blk(tile_a), plane_blk(tile_c),
                  full(vec), full(win3), full(cw3), full(wout3)],
        out_specs=pl.BlockSpec((tm, d), lambda i: (tile_c(i), 0)),
        out_shape=jax.ShapeDtypeStruct((nb * nj * S5_R, d), F32),
        scratch_shapes=[ubuf, ubuf, gate, gate,
                        pltpu.VMEM((n_strips, n_shift, S5_R, jt, LANES), F32),
                        pltpu.VMEM((n_strips, tm, LANES), F32),
                        _frame_scratch(d, tm),
                        pltpu.VMEM((tm, d), BF16),
                        pltpu.VMEM((3 * width // MM_COLS, tm, MM_COLS), F32),
                        pltpu.VMEM((tm, width), BF16),
                        pltpu.VMEM((d // MM_COLS, tm, MM_COLS), F32)],
        compiler_params=_cparams(("arbitrary",)),
        name="layer1",
    )(x3, x3, vec, win3, cw3, wout3)


def _pick_tile(n, pref):
    t = min(n, pref)
    while n % t:
        t //= 2
    return t


def kernel(x, norm_g, final_g, w_in_ab, s5_a_re, s5_a_im, s5_log_dt, s5_b_re, s5_b_im, s5_c_re, s5_c_im,
           s5_d, s5_glu_w, s5_glu_b, w_out_ab, w_in_c, conv_w, conv_b, conv_ln_g, conv_ln_b, w_out_c):
    nb, seq, d = x.shape
    rows = nb * seq
    s5_w = s5_glu_w.shape[-1]
    G = s5_w // S5_GROUP
    ret_qk = RET_HEADS * RET_DK
    ret_w = RET_HEADS * RET_DV
    nj = seq // S5_R
    x2 = x.reshape(rows, d)

    w_in = w_in_ab[0].astype(BF16)
    tm = _pick_tile(seq, 512)
    q, k, v, gr3, hn3, x3 = _proj0(x2, norm_g[0:1], w_in[:, 2 * s5_w:], nb, seq, ret_qk, ret_w, tm)
    at, g5t = _proj_u(hn3, w_in[:, :2 * s5_w].T, s5_w, 4)

    dup = lambda a: jnp.concatenate([a, a], axis=-1)
    are2 = dup(s5_a_re[0])[:, None, :]
    aim2 = dup(s5_a_im[0])[:, None, :]
    ldt = s5_log_dt[0][:, None, None]
    bre2 = dup(jnp.swapaxes(s5_b_re[0], 1, 2))
    bim2 = dup(jnp.swapaxes(s5_b_im[0], 1, 2))
    cre2 = dup(s5_c_re[0])
    cim2 = dup(s5_c_im[0])
    dsk_t = jnp.tile(s5_d[0].reshape(G, 1, S5_GROUP), (1, S5_R, 1)).reshape(G, S5_BLK, 1)
    mt, bsre, bsim, ccat, a16 = _s5_prep(are2, aim2, ldt, bre2, bim2, cre2, cim2, dsk_t)

    y5t = _s5_core(at, mt, bsre, bsim, ccat, a16)

    pos = jnp.arange(seq, dtype=F32)[:, None]
    half = RET_DK // 2
    freqs = ROPE_BASE ** (-jnp.arange(half, dtype=F32) / half)
    ang = pos * freqs[None, :]
    cos_h = jnp.concatenate([jnp.cos(ang), jnp.cos(ang)], axis=-1)
    sin_h = jnp.concatenate([-jnp.sin(ang), jnp.sin(ang)], axis=-1)
    cos_t = jnp.tile(cos_h, (1, RET_HEADS))
    sin_t = jnp.tile(sin_h, (1, RET_HEADS))
    log_g = jnp.log(1.0 - 2.0 ** (-5.0 - jnp.arange(RET_HEADS, dtype=F32)))
    idx = jnp.arange(RET_BLK, dtype=F32)
    chunk_of = jnp.arange(RET_BLK) // CHUNK
    seen = chunk_of[:, None] >= chunk_of[None, :]
    dm = jnp.where(seen[None], jnp.exp(log_g[:, None, None] * jnp.abs(idx[:, None] - idx[None, :])[None]), 0.0)
    qdec = jnp.exp(log_g[:, None] * (idx + 1.0)[None, :])[:, :, None]
    kdec = jnp.exp(log_g[:, None] * (RET_BLK - 1 - idx)[None, :])[:, :, None]
    cdec = jnp.broadcast_to(jnp.exp(log_g * RET_BLK)[:, None, None], (RET_HEADS, 1, RET_DV))
    yr3 = _retention(q, k, v, cos_t, sin_t, dm, qdec, kdec, cdec, nb, seq, tm)

    wo = w_out_ab[0].astype(BF16)
    x3 = _merge0(x3, y5t, g5t, yr3, gr3, s5_glu_w[0].T.astype(BF16), s5_glu_b[0][:, None],
                 wo[:s5_w], wo[s5_w:])

    out = _layer1(x3, norm_g[1:2], w_in_c[0].astype(BF16), conv_w[0], conv_b[0:1], conv_ln_g[0:1],
                  conv_ln_b[0:1], w_out_c[0].astype(BF16), final_g[None, :], tm)
    return out.reshape(nb, seq, d)
```

```python
import functools
import math

import jax
import jax.numpy as jnp
from jax import lax
from jax.experimental import pallas as pl
from jax.experimental.pallas import tpu as pltpu

F32 = jnp.float32
BF16 = jnp.bfloat16

EPS = 1e-6
CHUNK = 64
S5_GROUP = 16
S5_STATE = 64
S5_R = 16
S5_BLK = S5_R * S5_GROUP
RET_HEADS = 8
RET_DK = 64
RET_DV = 128
RET_BLK = 256
ROPE_BASE = 10000.0
CONV_K = 31
CONV_HALO = 32

LANES = 128
VMEM_LIMIT_BYTES = 56 * 1024 * 1024

_NT = (((1,), (1,)), ((), ()))
_TN = (((0,), (0,)), ((), ()))


def _cparams(sem):
    return pltpu.CompilerParams(dimension_semantics=sem, vmem_limit_bytes=VMEM_LIMIT_BYTES)


def _rms(x, g):
    return x * lax.rsqrt(jnp.mean(x * x, axis=-1, keepdims=True) + EPS) * g


FRAME_PITCH = 24
assert FRAME_PITCH >= S5_R and FRAME_PITCH % 8 == 0


def _frame_scratch(width, rows):
    return pltpu.VMEM((width // LANES, rows // S5_R * FRAME_PITCH, LANES), F32)


def _to_frames(val, scr_ref, dst_ref):
    nc = scr_ref.shape[0]
    jt = scr_ref.shape[1] // FRAME_PITCH
    for c in range(nc):
        for j in range(jt):
            scr_ref[c, j * FRAME_PITCH:j * FRAME_PITCH + S5_R, :] = val[j * S5_R:(j + 1) * S5_R, c * LANES:(c + 1) * LANES]
    for s in range(S5_R):
        dst_ref[0, s] = jnp.concatenate(
            [scr_ref[c, pl.ds(s, jt, stride=FRAME_PITCH), :] for c in range(nc)], axis=1).astype(dst_ref.dtype)


def _from_frames(val, scr_ref, dst_ref):
    nc = scr_ref.shape[0]
    jt = scr_ref.shape[1] // FRAME_PITCH
    for s in range(S5_R):
        for c in range(nc):
            scr_ref[c, pl.ds(s, jt, stride=FRAME_PITCH), :] = val[s * jt:(s + 1) * jt, c * LANES:(c + 1) * LANES]
    for j in range(jt):
        dst_ref[j * S5_R:(j + 1) * S5_R, :] = jnp.concatenate(
            [scr_ref[c, j * FRAME_PITCH:j * FRAME_PITCH + S5_R, :] for c in range(nc)], axis=1)


def _proj0_ret_kernel(x_ref, g_ref, w_ref, cos_ref, sin_ref, dm_ref, qdec_ref, kdec_ref, cdec_ref,
                      yr3_ref, gr3_ref, hn3_ref, x3_ref,
                      s_ref, qs_ref, ks_ref, vs_ref, o_ref, scr_ref, *, n_blocks):
    @pl.when(pl.program_id(1) == 0)
    def _():
        s_ref[...] = jnp.zeros(s_ref.shape, s_ref.dtype)

    qk_w, v_w = qs_ref.shape[-1], vs_ref.shape[-1]
    x = x_ref[...]
    hn32 = _rms(x, g_ref[...])
    hn = hn32.astype(BF16)
    q = jnp.dot(hn, w_ref[:, 0:qk_w], preferred_element_type=F32)
    k = jnp.dot(hn, w_ref[:, qk_w:2 * qk_w], preferred_element_type=F32)
    v = jnp.dot(hn, w_ref[:, 2 * qk_w:2 * qk_w + v_w], preferred_element_type=F32)
    gr = jnp.dot(hn, w_ref[:, 2 * qk_w + v_w:], preferred_element_type=F32)

    cos = cos_ref[...]
    sin = sin_ref[...]
    lane = lax.broadcasted_iota(jnp.int32, cos.shape, 1)
    first = (lane % RET_DK) < (RET_DK // 2)

    def rope(t):
        out = []
        for p in range(qk_w // LANES):
            tp = t[:, p * LANES:(p + 1) * LANES]
            partner = jnp.where(first, pltpu.roll(tp, LANES - RET_DK // 2, 1), pltpu.roll(tp, RET_DK // 2, 1))
            out.append(tp * cos + partner * sin)
        return jnp.concatenate(out, axis=1)

    qs_ref[...] = (rope(q) * (RET_DK ** -0.5)).astype(BF16)
    ks_ref[...] = rope(k).astype(BF16)
    vs_ref[...] = v.astype(BF16)
    _to_frames(hn32, scr_ref, hn3_ref)
    _to_frames(x, scr_ref, x3_ref)
    _to_frames(gr, scr_ref, gr3_ref)

    lane_p = lax.broadcasted_iota(jnp.int32, (RET_BLK, LANES), 1)
    heads = range(RET_HEADS)
    for c in range(n_blocks):
        rows = pl.ds(c * RET_BLK, RET_BLK)
        qp = [qs_ref[rows, pl.ds((h // 2) * LANES, LANES)] for h in heads]
        km = [jnp.where((lane_p // RET_DK) == (h % 2), ks_ref[rows, pl.ds((h // 2) * LANES, LANES)],
                        jnp.zeros((RET_BLK, LANES), BF16)) for h in heads]
        vh = [vs_ref[rows, pl.ds(h * RET_DV, RET_DV)] for h in heads]
        sc = [lax.dot_general(qp[h], km[h], _NT, preferred_element_type=F32) for h in heads]
        s_prev = [s_ref[h] for h in heads]
        cross = [jnp.dot(qp[h], s_prev[h].astype(BF16), preferred_element_type=F32) for h in heads]
        p = [(sc[h] * dm_ref[h]).astype(BF16) for h in heads]
        o = [jnp.dot(p[h], vh[h], preferred_element_type=F32) + qdec_ref[h] * cross[h] for h in heads]
        vd = [(vh[h].astype(F32) * kdec_ref[h]).astype(BF16) for h in heads]
        kv = [lax.dot_general(km[h], vd[h], _TN, preferred_element_type=F32) for h in heads]
        for h in heads:
            s_ref[h] = cdec_ref[h] * s_prev[h] + kv[h]
        mu = [jnp.mean(o[h], axis=-1, keepdims=True) for h in heads]
        var = [jnp.mean(jnp.square(o[h] - mu[h]), axis=-1, keepdims=True) for h in heads]
        for h in heads:
            res = (o[h] - mu[h]) * lax.rsqrt(var[h] + EPS)
            for j in range(RET_BLK // S5_R):
                j0 = (c * (RET_BLK // S5_R) + j) * FRAME_PITCH
                o_ref[h, j0:j0 + S5_R, :] = res[j * S5_R:(j + 1) * S5_R]
    jt = yr3_ref.shape[2]
    for s in range(S5_R):
        yr3_ref[0, s] = jnp.concatenate(
            [o_ref[h, pl.ds(s, jt, stride=FRAME_PITCH), :] for h in range(RET_HEADS)], axis=1).astype(yr3_ref.dtype)


def _proj0_ret(x2, g, w, cos, sin, dm, qdec, kdec, cdec, nb, seq, qk_w, v_w, tm):
    rows, d = x2.shape
    assert tm % RET_BLK == 0
    assert d == v_w, "the strided-read scratch is shared between hn, x and the retention gate"
    nt = seq // tm
    jt = tm // S5_R
    nj = seq // S5_R
    full = lambda a: pl.BlockSpec(a.shape, lambda b, t: (0,) * a.ndim)
    pos = pl.BlockSpec((tm, LANES), lambda b, t: (t, 0))
    fm = lambda wd: pl.BlockSpec((1, S5_R, jt, wd), lambda b, t: (b, 0, t, 0))
    fm_shape = lambda wd, dt: jax.ShapeDtypeStruct((nb, S5_R, nj, wd), dt)
    return pl.pallas_call(
        functools.partial(_proj0_ret_kernel, n_blocks=tm // RET_BLK),
        grid=(nb, nt),
        in_specs=[pl.BlockSpec((tm, d), lambda b, t: (b * nt + t, 0)), full(g), full(w), pos, pos,
                  full(dm), full(qdec), full(kdec), full(cdec)],
        out_specs=[fm(v_w), fm(v_w), fm(d), fm(d)],
        out_shape=[fm_shape(v_w, BF16), fm_shape(v_w, BF16), fm_shape(d, BF16), fm_shape(d, F32)],
        scratch_shapes=[pltpu.VMEM((RET_HEADS, LANES, RET_DV), F32),
                        pltpu.VMEM((tm, qk_w), BF16), pltpu.VMEM((tm, qk_w), BF16), pltpu.VMEM((tm, v_w), BF16),
                        _frame_scratch(v_w, tm), _frame_scratch(d, tm)],
        compiler_params=_cparams(("parallel", "arbitrary")),
        name="proj0_ret",
    )(x2, g, w, cos, sin, dm, qdec, kdec, cdec)


def _proj_u_kernel(hn3_ref, wt_ref, at_ref, g5t_ref, *, sg, s5_w):
    nj = hn3_ref.shape[2]
    for sl in range(sg):
        ut = lax.dot_general(wt_ref[...], hn3_ref[0, sl], _NT, preferred_element_type=F32)
        at_ref[0, :, sl * S5_GROUP:(sl + 1) * S5_GROUP, :] = (
            ut[:s5_w].reshape(s5_w // S5_GROUP, S5_GROUP, nj).astype(at_ref.dtype))
        g5t_ref[0, sl] = ut[s5_w:].astype(g5t_ref.dtype)


def _proj_u(hn3, wt, s5_w, sg):
    nb, _, nj, d = hn3.shape
    G = s5_w // S5_GROUP
    return pl.pallas_call(
        functools.partial(_proj_u_kernel, sg=sg, s5_w=s5_w),
        grid=(nb, S5_R // sg),
        in_specs=[
            pl.BlockSpec((1, sg, nj, d), lambda b, s: (b, s, 0, 0)),
            pl.BlockSpec(wt.shape, lambda b, s: (0, 0)),
        ],
        out_specs=[
            pl.BlockSpec((1, G, sg * S5_GROUP, nj), lambda b, s: (b, 0, s, 0)),
            pl.BlockSpec((1, sg, s5_w, nj), lambda b, s: (b, s, 0, 0)),
        ],
        out_shape=[jax.ShapeDtypeStruct((nb, G, S5_BLK, nj), BF16),
                   jax.ShapeDtypeStruct((nb, S5_R, s5_w, nj), BF16)],
        compiler_params=_cparams(("parallel", "parallel")),
        name="proj_u",
    )(hn3, wt)


def _s5_prep_kernel(are_ref, aim_ref, ldt_ref, bre_ref, bim_ref, cre_ref, cim_ref, dsk_ref,
                    mt_ref, bsre_ref, bsim_ref, ccat_ref, a16_ref):
    g = pl.program_id(0)
    are = are_ref[0]
    aim = aim_ref[0]
    dt = jnp.exp(ldt_ref[0])
    bre, bim = bre_ref[0], bim_ref[0]
    cre, cim = cre_ref[0], cim_ref[0]

    def lpow(e):
        mag = jnp.exp(are * dt * e)
        ang = aim * dt * e
        return mag * jnp.cos(ang), mag * jnp.sin(ang)

    one = jnp.ones((1, 1), F32)
    lb_re, lb_im = lpow(one)
    nr, ni = lb_re - 1.0, lb_im
    den = are * are + aim * aim
    coef_re = (nr * are + ni * aim) / den
    coef_im = (ni * are - nr * aim) / den
    bb_re = coef_re * bre - coef_im * bim
    bb_im = coef_re * bim + coef_im * bre

    p = lax.broadcasted_iota(jnp.int32, (S5_R, 1), 0).astype(F32)
    lane = lax.broadcasted_iota(jnp.int32, (1, LANES), 1)
    mine = (lane // S5_STATE) == (g % 2)
    low = lane < S5_STATE

    def outer(t_re, t_im, m_re, m_im):
        re = t_re[:, None, :] * m_re[None, :, :] - t_im[:, None, :] * m_im[None, :, :]
        im = t_re[:, None, :] * m_im[None, :, :] + t_im[:, None, :] * m_re[None, :, :]
        return re.reshape(S5_BLK, LANES), im.reshape(S5_BLK, LANES)

    half = float(S5_R // 2)
    e_re, e_im = lpow(p - half)
    f_re, f_im = lpow(half - p)
    p_re, p_im = outer(e_re, e_im, cre, cim)
    q_re, q_im = outer(f_re, f_im, bb_re, bb_im)
    q_re = jnp.where(low, q_re, 0.0)
    q_im = jnp.where(low, q_im, 0.0)
    mt = (lax.dot_general(p_re, q_re, _NT, precision=lax.Precision.HIGHEST, preferred_element_type=F32)
          - lax.dot_general(p_im, q_im, _NT, precision=lax.Precision.HIGHEST, preferred_element_type=F32))
    row = lax.broadcasted_iota(jnp.int32, (S5_BLK, S5_BLK), 0)
    col = lax.broadcasted_iota(jnp.int32, (S5_BLK, S5_BLK), 1)
    mt = jnp.where(row // S5_GROUP >= col // S5_GROUP, mt, 0.0)
    mt = mt + jnp.where(row == col, dsk_ref[0], 0.0)
    mt_ref[0] = mt.astype(mt_ref.dtype)

    g_re, g_im = lpow(float(S5_R - 1) - p)
    bs_re, bs_im = outer(g_re, g_im, bb_re, bb_im)
    bsre_ref[0] = jnp.where(mine, bs_re, 0.0).astype(bsre_ref.dtype)
    bsim_ref[0] = jnp.where(mine, bs_im, 0.0).astype(bsim_ref.dtype)

    h_re, h_im = lpow(p + 1.0)
    cs_re, cs_im = outer(h_re, h_im, cre, cim)
    ccat_ref[0, :, 0:LANES] = jnp.where(mine, cs_re, 0.0).astype(ccat_ref.dtype)
    ccat_ref[0, :, LANES:2 * LANES] = jnp.where(mine, -cs_im, 0.0).astype(ccat_ref.dtype)

    a_re, a_im = lpow(one * float(S5_R))
    rows8 = lax.broadcasted_iota(jnp.int32, (8, LANES), 0)
    a16_ref[0] = jnp.where(mine, jnp.where(rows8 == 0, a_re, jnp.where(rows8 == 1, a_im, 0.0)), 0.0)


def _s5_prep(are2, aim2, ldt, bre2, bim2, cre2, cim2, dsk_t):
    G = are2.shape[0]
    spec3 = lambda shape: pl.BlockSpec((1,) + shape, lambda g: (g, 0, 0))
    return pl.pallas_call(
        _s5_prep_kernel,
        grid=(G,),
        in_specs=[spec3((1, LANES)), spec3((1, LANES)), spec3((1, 1)),
                  spec3((S5_GROUP, LANES)), spec3((S5_GROUP, LANES)),
                  spec3((S5_GROUP, LANES)), spec3((S5_GROUP, LANES)), spec3((S5_BLK, 1))],
        out_specs=[spec3((S5_BLK, S5_BLK)), spec3((S5_BLK, LANES)), spec3((S5_BLK, LANES)),
                   spec3((S5_BLK, 2 * LANES)), spec3((8, LANES))],
        out_shape=[jax.ShapeDtypeStruct((G, S5_BLK, S5_BLK), BF16),
                   jax.ShapeDtypeStruct((G, S5_BLK, LANES), BF16),
                   jax.ShapeDtypeStruct((G, S5_BLK, LANES), BF16),
                   jax.ShapeDtypeStruct((G, S5_BLK, 2 * LANES), BF16),
                   jax.ShapeDtypeStruct((G, 8, LANES), F32)],
        compiler_params=_cparams(("parallel",)),
        name="s5_prep",
    )(are2, aim2, ldt, bre2, bim2, cre2, cim2, dsk_t)


def _s5_core_kernel(at_ref, mt_ref, bsre_ref, bsim_ref, ccat_ref, a16_ref, yt_ref,
                    zre_ref, zim_ref, xre_ref, xim_ref, yloc_ref, *, nb, nj):
    for b in range(nb):
        rows = pl.ds(b * nj, nj)
        zre_ref[rows, :] = (lax.dot_general(at_ref[b, 0], bsre_ref[0], _TN, preferred_element_type=F32)
                            + lax.dot_general(at_ref[b, 1], bsre_ref[1], _TN, preferred_element_type=F32))
        zim_ref[rows, :] = (lax.dot_general(at_ref[b, 0], bsim_ref[0], _TN, preferred_element_type=F32)
                            + lax.dot_general(at_ref[b, 1], bsim_ref[1], _TN, preferred_element_type=F32))
    a_re = jnp.broadcast_to(a16_ref[0, 0:1, :] + a16_ref[1, 0:1, :], (nb, LANES))
    a_im = jnp.broadcast_to(a16_ref[0, 1:2, :] + a16_ref[1, 1:2, :], (nb, LANES))

    def step(j, carry):
        xr, xi = carry
        idx = pl.ds(j, nb, stride=nj)
        xre_ref[idx, :] = xr
        xim_ref[idx, :] = xi
        zr = zre_ref[idx, :]
        zi = zim_ref[idx, :]
        return a_re * xr - a_im * xi + zr, a_re * xi + a_im * xr + zi

    for b in range(nb):
        for i in range(2):
            yloc_ref[b, i] = jnp.dot(mt_ref[i], at_ref[b, i], preferred_element_type=F32)
    carry = (jnp.zeros((nb, LANES), F32), jnp.zeros((nb, LANES), F32))
    for j in range(nj):
        carry = step(j, carry)
    xcat = [jnp.concatenate([xre_ref[pl.ds(b * nj, nj), :], xim_ref[pl.ds(b * nj, nj), :]], axis=1).astype(BF16)
            for b in range(nb)]
    cross = [[lax.dot_general(ccat_ref[i], xcat[b], _NT, preferred_element_type=F32) for i in range(2)]
             for b in range(nb)]
    for b in range(nb):
        for i in range(2):
            yt_ref[b, i] = (yloc_ref[b, i] + cross[b][i]).astype(yt_ref.dtype)


def _s5_core(at, mt, bsre, bsim, ccat, a16):
    nb, G, _, nj = at.shape
    pair = lambda shape: pl.BlockSpec((2,) + shape, lambda p: (p, 0, 0))
    data = pl.BlockSpec((nb, 2, S5_BLK, nj), lambda p: (0, p, 0, 0))
    return pl.pallas_call(
        functools.partial(_s5_core_kernel, nb=nb, nj=nj),
        grid=(G // 2,),
        in_specs=[data, pair((S5_BLK, S5_BLK)), pair((S5_BLK, LANES)),
                  pair((S5_BLK, LANES)), pair((S5_BLK, 2 * LANES)), pair((8, LANES))],
        out_specs=data,
        out_shape=jax.ShapeDtypeStruct(at.shape, BF16),
        scratch_shapes=[pltpu.VMEM((nb * nj, LANES), F32)] * 4 + [pltpu.VMEM((nb, 2, S5_BLK, nj), F32)],
        compiler_params=_cparams(("parallel",)),
        name="s5_core",
    )(at, mt, bsre, bsim, ccat, a16)


MERGE_CB = 256
def _merge0_kernel(x_ref, y5t_ref, g5t_ref, yr_ref, gr_ref, gwt_ref, gb_ref, wo5_ref, wor_ref, o_ref):
    nj = x_ref.shape[1]
    s5_w = gwt_ref.shape[0]
    y = jax.nn.gelu(y5t_ref[0].astype(F32).reshape(-1, nj))
    yb = y.astype(BF16)
    r = yr_ref[0].astype(F32) * jax.nn.silu(gr_ref[0].astype(F32))
    blocks = [pl.ds(c0, MERGE_CB) for c0 in range(0, s5_w, MERGE_CB)]
    gates = [jnp.dot(gwt_ref[cb, :], yb, preferred_element_type=F32) + gb_ref[cb, :] for cb in blocks]
    out = x_ref[0] + jnp.dot(r.astype(BF16), wor_ref[...], preferred_element_type=F32)
    for n, cb in enumerate(blocks):
        a = (y[n * MERGE_CB:(n + 1) * MERGE_CB] * jax.nn.sigmoid(gates[n])
             * jax.nn.silu(g5t_ref[0, cb, :].astype(F32)))
        out = out + lax.dot_general(a.astype(BF16), wo5_ref[cb, :], _TN, preferred_element_type=F32)
    o_ref[0] = out


def _merge0(x3, y5t, g5t, yr3, gr3, gwt, gb_col, wo5, wor):
    nb, _, nj, d = x3.shape
    G = y5t.shape[1]
    s5_w = g5t.shape[2]
    ret_w = wor.shape[0]
    y5t5 = y5t.reshape(nb, G, S5_R, S5_GROUP, nj)
    strided = lambda wd: pl.BlockSpec((1, None, nj, wd), lambda b, s: (b, s, 0, 0))
    full = lambda a: pl.BlockSpec(a.shape, lambda b, s: (0,) * a.ndim)
    return pl.pallas_call(
        _merge0_kernel,
        grid=(nb, S5_R),
        in_specs=[strided(d),
                  pl.BlockSpec((1, G, None, S5_GROUP, nj), lambda b, s: (b, 0, s, 0, 0)),
                  pl.BlockSpec((1, None, s5_w, nj), lambda b, s: (b, s, 0, 0)),
                  strided(ret_w), strided(ret_w), full(gwt), full(gb_col), full(wo5), full(wor)],
        out_specs=strided(d),
        out_shape=jax.ShapeDtypeStruct(x3.shape, F32),
        compiler_params=_cparams(("parallel", "parallel")),
        name="merge0",
    )(x3, y5t5, g5t, yr3, gr3, gwt, gb_col, wo5, wor)


CONV_CARRY = 8
CONV_TB = 8
MM_COLS = 256
LOOP_TRIPS = 4
assert (CONV_K - 1 + S5_R - 1) // S5_R <= CONV_CARRY


def _layer1_kernel(xa_ref, xc_ref, vec_ref, win_ref, cw_ref, wout_ref, o_ref,
                   ubuf0_ref, ubuf1_ref, gate0_ref, gate1_ref, sh_ref, acc_ref, scr_ref,
                   hn_ref, pr_ref, u_ref, od_ref, *, jt, width, nt):
    i = pl.program_id(0)
    rows = S5_R * jt
    d = xa_ref.shape[-1]
    n_shift = sh_ref.shape[1]
    n_strips = acc_ref.shape[0]
    n_proj, n_out = win_ref.shape[0], wout_ref.shape[0]
    n_ab = 2 * width // MM_COLS
    assert n_strips % LOOP_TRIPS == 0 and n_proj % LOOP_TRIPS == 0 and n_out == LOOP_TRIPS

    @pl.when(i == 0)
    def _():
        ubuf1_ref[...] = jnp.zeros(ubuf1_ref.shape, F32)
        gate1_ref[...] = jnp.zeros(gate1_ref.shape, F32)
        u_ref[...] = jnp.zeros(u_ref.shape, u_ref.dtype)

    def conv_strip(ubuf_b, c):
        for dl in range(1, n_shift + 1):
            sh_ref[c, dl - 1] = ubuf_b[c, :, CONV_CARRY - dl:CONV_CARRY - dl + jt, :]
        for s0 in range(0, S5_R, CONV_TB):
            targets = range(s0, s0 + CONV_TB)
            accs = {s: jnp.broadcast_to(cw_ref[c, CONV_K:CONV_K + 1, :], (jt, LANES)) for s in targets}
            for e in range(s0 - (CONV_K - 1), s0 + CONV_TB):
                plane, dl = e % S5_R, -(e // S5_R)
                src = ubuf_b[c, plane, CONV_CARRY:CONV_CARRY + jt, :] if dl == 0 else sh_ref[c, dl - 1, plane]
                for s in targets:
                    kk = e + (CONV_K - 1) - s
                    if 0 <= kk < CONV_K:
                        accs[s] = accs[s] + src * cw_ref[c, kk:kk + 1, :]
            for s in targets:
                acc_ref[c, s * jt:(s + 1) * jt, :] = accs[s]

    def step(ubuf_a, gate_a, ubuf_b, gate_b):
        hn_ref[...] = _rms(xa_ref[0].reshape(rows, d), vec_ref[0:1, :]).astype(BF16)

        def body(it, carry):
            for k in range(n_strips // LOOP_TRIPS):
                conv_strip(ubuf_b, it * (n_strips // LOOP_TRIPS) + k)
            for k in range(n_proj // LOOP_TRIPS):
                n = it * (n_proj // LOOP_TRIPS) + k
                pr_ref[n] = jnp.dot(hn_ref[...], win_ref[n], preferred_element_type=F32)
            od_ref[it] = jnp.dot(u_ref[...], wout_ref[it], preferred_element_type=F32)
            return carry

        for it in range(LOOP_TRIPS):
            @pl.when(i >= 0)
            def _(it=it):
                body(it, 0)

        x1 = xc_ref[0].reshape(rows, d) + jnp.concatenate([od_ref[n] for n in range(n_out)], axis=1)
        _from_frames(_rms(x1, vec_ref[3:4, :]), scr_ref, o_ref)

        acc = jnp.concatenate([acc_ref[c] for c in range(n_strips)], axis=1)
        mu = jnp.mean(acc, axis=-1, keepdims=True)
        var = jnp.mean(jnp.square(acc - mu), axis=-1, keepdims=True)
        uf = (acc - mu) * lax.rsqrt(var + EPS) * vec_ref[1:2, :] + vec_ref[2:3, :]
        gate = jnp.concatenate([gate_b[n] for n in range(gate_b.shape[0])], axis=1)
        u_ref[...] = (jax.nn.silu(uf) * jax.nn.silu(gate)).astype(u_ref.dtype)

        first = (i % nt) == 0
        for n in range(n_proj - n_ab):
            gate_a[n] = pr_ref[n_ab + n]
        for n in range(n_ab // 2):
            glu = pr_ref[n] * jax.nn.sigmoid(pr_ref[n_ab // 2 + n])
            for k in range(MM_COLS // LANES):
                c = n * (MM_COLS // LANES) + k
                ubuf_a[c, :, CONV_CARRY:CONV_CARRY + jt, :] = glu[:, k * LANES:(k + 1) * LANES].reshape(S5_R, jt, LANES)
        for c in range(n_strips):
            hist = ubuf_b[c, :, jt:jt + CONV_CARRY, :]
            ubuf_a[c, :, 0:CONV_CARRY, :] = jnp.where(first, jnp.zeros_like(hist), hist)

    @pl.when(i % 2 == 0)
    def _():
        step(ubuf0_ref, gate0_ref, ubuf1_ref, gate1_ref)

    @pl.when(i % 2 == 1)
    def _():
        step(ubuf1_ref, gate1_ref, ubuf0_ref, gate0_ref)


def _layer1(x3, g, win, cw, cb, lg, lb, wout, fg, tm):
    nb, _, nj, d = x3.shape
    width = wout.shape[0]
    jt = tm // S5_R
    nt = nj // jt
    n_tiles = nb * nt
    n_strips = width // LANES
    n_shift = (CONV_K - 1 + S5_R - 1) // S5_R
    win3 = win.reshape(d, 3 * width // MM_COLS, MM_COLS).transpose(1, 0, 2)
    wout3 = wout.reshape(width, d // MM_COLS, MM_COLS).transpose(1, 0, 2)
    cw3 = jnp.concatenate([cw, cb], axis=0).reshape(CONV_K + 1, n_strips, LANES).transpose(1, 0, 2)
    vec = jnp.concatenate([g, lg, lb, fg, jnp.zeros((4, d), F32)], axis=0)
    full = lambda a: pl.BlockSpec(a.shape, lambda i: (0,) * a.ndim, pipeline_mode=pl.Buffered(1))
    tile_a = lambda i: jnp.minimum(i, n_tiles - 1)
    tile_c = lambda i: jnp.clip(i - 2, 0, n_tiles - 1)
    plane_blk = lambda tile: pl.BlockSpec((1, S5_R, jt, d), lambda i: (tile(i) // nt, 0, tile(i) % nt, 0))
    ubuf = pltpu.VMEM((n_strips, S5_R, CONV_CARRY + jt, LANES), F32)
    gate = pltpu.VMEM((width // MM_COLS, tm, MM_COLS), F32)
    return pl.pallas_call(
        functools.partial(_layer1_kernel, jt=jt, width=width, nt=nt),
        grid=(n_tiles + 2,),
        in_specs=[plane_blk(tile_a), plane_blk(tile_c),
                  full(vec), full(win3), full(cw3), full(wout3)],
        out_specs=pl.BlockSpec((tm, d), lambda i: (tile_c(i), 0)),
        out_shape=jax.ShapeDtypeStruct((nb * nj * S5_R, d), F32),
        scratch_shapes=[ubuf, ubuf, gate, gate,
                        pltpu.VMEM((n_strips, n_shift, S5_R, jt, LANES), F32),
                        pltpu.VMEM((n_strips, tm, LANES), F32),
                        _frame_scratch(d, tm),
                        pltpu.VMEM((tm, d), BF16),
                        pltpu.VMEM((3 * width // MM_COLS, tm, MM_COLS), F32),
                        pltpu.VMEM((tm, width), BF16),
                        pltpu.VMEM((d // MM_COLS, tm, MM_COLS), F32)],
        compiler_params=_cparams(("arbitrary",)),
        name="layer1",
    )(x3, x3, vec, win3, cw3, wout3)


def _pick_tile(n, pref):
    t = min(n, pref)
    while n % t:
        t //= 2
    return t


def kernel(x, norm_g, final_g, w_in_ab, s5_a_re, s5_a_im, s5_log_dt, s5_b_re, s5_b_im, s5_c_re, s5_c_im,
           s5_d, s5_glu_w, s5_glu_b, w_out_ab, w_in_c, conv_w, conv_b, conv_ln_g, conv_ln_b, w_out_c):
    nb, seq, d = x.shape
    rows = nb * seq
    s5_w = s5_glu_w.shape[-1]
    G = s5_w // S5_GROUP
    ret_qk = RET_HEADS * RET_DK
    ret_w = RET_HEADS * RET_DV
    nj = seq // S5_R
    x2 = x.reshape(rows, d)

    w_in = w_in_ab[0].astype(BF16)
    tm = _pick_tile(seq, 512)

    dup = lambda a: jnp.concatenate([a, a], axis=-1)
    are2 = dup(s5_a_re[0])[:, None, :]
    aim2 = dup(s5_a_im[0])[:, None, :]
    ldt = s5_log_dt[0][:, None, None]
    bre2 = dup(jnp.swapaxes(s5_b_re[0], 1, 2))
    bim2 = dup(jnp.swapaxes(s5_b_im[0], 1, 2))
    cre2 = dup(s5_c_re[0])
    cim2 = dup(s5_c_im[0])
    dsk_t = jnp.tile(s5_d[0].reshape(G, 1, S5_GROUP), (1, S5_R, 1)).reshape(G, S5_BLK, 1)
    mt, bsre, bsim, ccat, a16 = _s5_prep(are2, aim2, ldt, bre2, bim2, cre2, cim2, dsk_t)

    pos = jnp.arange(seq, dtype=F32)[:, None]
    half = RET_DK // 2
    freqs = ROPE_BASE ** (-jnp.arange(half, dtype=F32) / half)
    ang = pos * freqs[None, :]
    cos_t = jnp.tile(jnp.concatenate([jnp.cos(ang), jnp.cos(ang)], axis=-1), (1, LANES // RET_DK))
    sin_t = jnp.tile(jnp.concatenate([-jnp.sin(ang), jnp.sin(ang)], axis=-1), (1, LANES // RET_DK))
    log_g = jnp.log(1.0 - 2.0 ** (-5.0 - jnp.arange(RET_HEADS, dtype=F32)))
    idx = jnp.arange(RET_BLK, dtype=F32)
    chunk_of = jnp.arange(RET_BLK) // CHUNK
    seen = chunk_of[:, None] >= chunk_of[None, :]
    dm = jnp.where(seen[None], jnp.exp(log_g[:, None, None] * jnp.abs(idx[:, None] - idx[None, :])[None]), 0.0)
    qdec = jnp.exp(log_g[:, None] * (idx + 1.0)[None, :])[:, :, None]
    kdec = jnp.exp(log_g[:, None] * (RET_BLK - 1 - idx)[None, :])[:, :, None]
    cdec = jnp.broadcast_to(jnp.exp(log_g * RET_BLK)[:, None, None], (RET_HEADS, 1, RET_DV))
    yr3, gr3, hn3, x3 = _proj0_ret(x2, norm_g[0:1], w_in[:, 2 * s5_w:], cos_t, sin_t, dm, qdec, kdec, cdec,
                                   nb, seq, ret_qk, ret_w, tm)
    at, g5t = _proj_u(hn3, w_in[:, :2 * s5_w].T, s5_w, 4)
    y5t = _s5_core(at, mt, bsre, bsim, ccat, a16)

    wo = w_out_ab[0].astype(BF16)
    x3 = _merge0(x3, y5t, g5t, yr3, gr3, s5_glu_w[0].T.astype(BF16), s5_glu_b[0][:, None],
                 wo[:s5_w], wo[s5_w:])

    out = _layer1(x3, norm_g[1:2], w_in_c[0].astype(BF16), conv_w[0], conv_b[0:1], conv_ln_g[0:1],
                  conv_ln_b[0:1], w_out_c[0].astype(BF16), final_g[None, :], tm)
    return out.reshape(nb, seq, d)
```

```python
import functools
import math

import jax
import jax.numpy as jnp
from jax import lax
from jax.experimental import pallas as pl
from jax.experimental.pallas import tpu as pltpu

F32 = jnp.float32
BF16 = jnp.bfloat16

EPS = 1e-6
CHUNK = 64
S5_GROUP = 16
S5_STATE = 64
S5_R = 16
S5_BLK = S5_R * S5_GROUP
RET_HEADS = 8
RET_DK = 64
RET_DV = 128
RET_BLK = 256
ROPE_BASE = 10000.0
CONV_K = 31
CONV_HALO = 32

LANES = 128
VMEM_LIMIT_BYTES = 56 * 1024 * 1024

_NT = (((1,), (1,)), ((), ()))
_TN = (((0,), (0,)), ((), ()))


def _cparams(sem):
    return pltpu.CompilerParams(dimension_semantics=sem, vmem_limit_bytes=VMEM_LIMIT_BYTES)


def _rms(x, g):
    return x * lax.rsqrt(jnp.mean(x * x, axis=-1, keepdims=True) + EPS) * g


FRAME_PITCH = 24
assert FRAME_PITCH >= S5_R and FRAME_PITCH % 8 == 0


def _frame_scratch(width, rows):
    return pltpu.VMEM((width // LANES, rows // S5_R * FRAME_PITCH, LANES), F32)


def _to_frames(val, scr_ref, dst_ref):
    nc = scr_ref.shape[0]
    jt = scr_ref.shape[1] // FRAME_PITCH
    for c in range(nc):
        for j in range(jt):
            scr_ref[c, j * FRAME_PITCH:j * FRAME_PITCH + S5_R, :] = val[j * S5_R:(j + 1) * S5_R, c * LANES:(c + 1) * LANES]
    for s in range(S5_R):
        dst_ref[0, s] = jnp.concatenate(
            [scr_ref[c, pl.ds(s, jt, stride=FRAME_PITCH), :] for c in range(nc)], axis=1).astype(dst_ref.dtype)


def _from_frames(val, scr_ref, dst_ref):
    nc = scr_ref.shape[0]
    jt = scr_ref.shape[1] // FRAME_PITCH
    for s in range(S5_R):
        for c in range(nc):
            scr_ref[c, pl.ds(s, jt, stride=FRAME_PITCH), :] = val[s * jt:(s + 1) * jt, c * LANES:(c + 1) * LANES]
    for j in range(jt):
        dst_ref[j * S5_R:(j + 1) * S5_R, :] = jnp.concatenate(
            [scr_ref[c, j * FRAME_PITCH:j * FRAME_PITCH + S5_R, :] for c in range(nc)], axis=1)


def _proj0_ret_kernel(x_ref, g_ref, w_ref, cos_ref, sin_ref, dm_ref, qdec_ref, kdec_ref, cdec_ref,
                      yr3_ref, gr3_ref, hn3_ref, x3_ref,
                      s_ref, qs_ref, ks_ref, vs_ref, o_ref, scr_ref, *, n_blocks):
    @pl.when(pl.program_id(1) == 0)
    def _():
        s_ref[...] = jnp.zeros(s_ref.shape, s_ref.dtype)

    qk_w, v_w = qs_ref.shape[-1], vs_ref.shape[-1]
    x = x_ref[...]
    hn32 = _rms(x, g_ref[...])
    hn = hn32.astype(BF16)
    q = jnp.dot(hn, w_ref[:, 0:qk_w], preferred_element_type=F32)
    k = jnp.dot(hn, w_ref[:, qk_w:2 * qk_w], preferred_element_type=F32)
    v = jnp.dot(hn, w_ref[:, 2 * qk_w:2 * qk_w + v_w], preferred_element_type=F32)
    gr = jnp.dot(hn, w_ref[:, 2 * qk_w + v_w:], preferred_element_type=F32)

    cos = cos_ref[...]
    sin = sin_ref[...]
    lane = lax.broadcasted_iota(jnp.int32, cos.shape, 1)
    first = (lane % RET_DK) < (RET_DK // 2)

    def rope(t):
        out = []
        for p in range(qk_w // LANES):
            tp = t[:, p * LANES:(p + 1) * LANES]
            partner = jnp.where(first, pltpu.roll(tp, LANES - RET_DK // 2, 1), pltpu.roll(tp, RET_DK // 2, 1))
            out.append(tp * cos + partner * sin)
        return jnp.concatenate(out, axis=1)

    qs_ref[...] = (rope(q) * (RET_DK ** -0.5)).astype(BF16)
    ks_ref[...] = rope(k).astype(BF16)
    vs_ref[...] = v.astype(BF16)
    _to_frames(hn32, scr_ref, hn3_ref)
    _to_frames(x, scr_ref, x3_ref)
    _to_frames(gr, scr_ref, gr3_ref)

    lane_p = lax.broadcasted_iota(jnp.int32, (RET_BLK, LANES), 1)
    heads = range(RET_HEADS)
    for c in range(n_blocks):
        rows = pl.ds(c * RET_BLK, RET_BLK)
        qp = [qs_ref[rows, pl.ds((h // 2) * LANES, LANES)] for h in heads]
        km = [jnp.where((lane_p // RET_DK) == (h % 2), ks_ref[rows, pl.ds((h // 2) * LANES, LANES)],
                        jnp.zeros((RET_BLK, LANES), BF16)) for h in heads]
        vh = [vs_ref[rows, pl.ds(h * RET_DV, RET_DV)] for h in heads]
        sc = [lax.dot_general(qp[h], km[h], _NT, preferred_element_type=F32) for h in heads]
        s_prev = [s_ref[h] for h in heads]
        cross = [jnp.dot(qp[h], s_prev[h].astype(BF16), preferred_element_type=F32) for h in heads]
        p = [(sc[h] * dm_ref[h]).astype(BF16) for h in heads]
        o = [jnp.dot(p[h], vh[h], preferred_element_type=F32) + qdec_ref[h] * cross[h] for h in heads]
        vd = [(vh[h].astype(F32) * kdec_ref[h]).astype(BF16) for h in heads]
        kv = [lax.dot_general(km[h], vd[h], _TN, preferred_element_type=F32) for h in heads]
        for h in heads:
            s_ref[h] = cdec_ref[h] * s_prev[h] + kv[h]
        mu = [jnp.mean(o[h], axis=-1, keepdims=True) for h in heads]
        var = [jnp.mean(jnp.square(o[h] - mu[h]), axis=-1, keepdims=True) for h in heads]
        for h in heads:
            res = (o[h] - mu[h]) * lax.rsqrt(var[h] + EPS)
            for j in range(RET_BLK // S5_R):
                j0 = (c * (RET_BLK // S5_R) + j) * FRAME_PITCH
                o_ref[h, j0:j0 + S5_R, :] = res[j * S5_R:(j + 1) * S5_R]
    jt = yr3_ref.shape[2]
    for s in range(S5_R):
        yr3_ref[0, s] = jnp.concatenate(
            [o_ref[h, pl.ds(s, jt, stride=FRAME_PITCH), :] for h in range(RET_HEADS)], axis=1).astype(yr3_ref.dtype)


def _proj0_ret(x2, g, w, cos, sin, dm, qdec, kdec, cdec, nb, seq, qk_w, v_w, tm):
    rows, d = x2.shape
    assert tm % RET_BLK == 0
    assert d == v_w, "the strided-read scratch is shared between hn, x and the retention gate"
    nt = seq // tm
    jt = tm // S5_R
    nj = seq // S5_R
    full = lambda a: pl.BlockSpec(a.shape, lambda b, t: (0,) * a.ndim)
    pos = pl.BlockSpec((tm, LANES), lambda b, t: (t, 0))
    fm = lambda wd: pl.BlockSpec((1, S5_R, jt, wd), lambda b, t: (b, 0, t, 0))
    fm_shape = lambda wd, dt: jax.ShapeDtypeStruct((nb, S5_R, nj, wd), dt)
    return pl.pallas_call(
        functools.partial(_proj0_ret_kernel, n_blocks=tm // RET_BLK),
        grid=(nb, nt),
        in_specs=[pl.BlockSpec((tm, d), lambda b, t: (b * nt + t, 0)), full(g), full(w), pos, pos,
                  full(dm), full(qdec), full(kdec), full(cdec)],
        out_specs=[fm(v_w), fm(v_w), fm(d), fm(d)],
        out_shape=[fm_shape(v_w, BF16), fm_shape(v_w, BF16), fm_shape(d, BF16), fm_shape(d, F32)],
        scratch_shapes=[pltpu.VMEM((RET_HEADS, LANES, RET_DV), F32),
                        pltpu.VMEM((tm, qk_w), BF16), pltpu.VMEM((tm, qk_w), BF16), pltpu.VMEM((tm, v_w), BF16),
                        _frame_scratch(v_w, tm), _frame_scratch(d, tm)],
        compiler_params=_cparams(("parallel", "arbitrary")),
        name="proj0_ret",
    )(x2, g, w, cos, sin, dm, qdec, kdec, cdec)


def _proj_u_kernel(hn3_ref, wt_ref, at_ref, g5t_ref, *, sg, s5_w):
    nj = hn3_ref.shape[2]
    for sl in range(sg):
        ut = lax.dot_general(wt_ref[...], hn3_ref[0, sl], _NT, preferred_element_type=F32)
        at_ref[0, :, sl * S5_GROUP:(sl + 1) * S5_GROUP, :] = (
            ut[:s5_w].reshape(s5_w // S5_GROUP, S5_GROUP, nj).astype(at_ref.dtype))
        g5t_ref[0, sl] = ut[s5_w:].astype(g5t_ref.dtype)


def _proj_u(hn3, wt, s5_w, sg):
    nb, _, nj, d = hn3.shape
    G = s5_w // S5_GROUP
    return pl.pallas_call(
        functools.partial(_proj_u_kernel, sg=sg, s5_w=s5_w),
        grid=(nb, S5_R // sg),
        in_specs=[
            pl.BlockSpec((1, sg, nj, d), lambda b, s: (b, s, 0, 0)),
            pl.BlockSpec(wt.shape, lambda b, s: (0, 0)),
        ],
        out_specs=[
            pl.BlockSpec((1, G, sg * S5_GROUP, nj), lambda b, s: (b, 0, s, 0)),
            pl.BlockSpec((1, sg, s5_w, nj), lambda b, s: (b, s, 0, 0)),
        ],
        out_shape=[jax.ShapeDtypeStruct((nb, G, S5_BLK, nj), BF16),
                   jax.ShapeDtypeStruct((nb, S5_R, s5_w, nj), BF16)],
        compiler_params=_cparams(("parallel", "parallel")),
        name="proj_u",
    )(hn3, wt)


def _s5_prep_kernel(are_ref, aim_ref, ldt_ref, bre_ref, bim_ref, cre_ref, cim_ref, dsk_ref,
                    mt_ref, bsre_ref, bsim_ref, ccat_ref, a16_ref):
    g = pl.program_id(0)
    are = are_ref[0]
    aim = aim_ref[0]
    dt = jnp.exp(ldt_ref[0])
    bre, bim = bre_ref[0], bim_ref[0]
    cre, cim = cre_ref[0], cim_ref[0]

    def lpow(e):
        mag = jnp.exp(are * dt * e)
        ang = aim * dt * e
        return mag * jnp.cos(ang), mag * jnp.sin(ang)

    one = jnp.ones((1, 1), F32)
    lb_re, lb_im = lpow(one)
    nr, ni = lb_re - 1.0, lb_im
    den = are * are + aim * aim
    coef_re = (nr * are + ni * aim) / den
    coef_im = (ni * are - nr * aim) / den
    bb_re = coef_re * bre - coef_im * bim
    bb_im = coef_re * bim + coef_im * bre

    p = lax.broadcasted_iota(jnp.int32, (S5_R, 1), 0).astype(F32)
    lane = lax.broadcasted_iota(jnp.int32, (1, LANES), 1)
    mine = (lane // S5_STATE) == (g % 2)
    low = lane < S5_STATE

    def outer(t_re, t_im, m_re, m_im):
        re = t_re[:, None, :] * m_re[None, :, :] - t_im[:, None, :] * m_im[None, :, :]
        im = t_re[:, None, :] * m_im[None, :, :] + t_im[:, None, :] * m_re[None, :, :]
        return re.reshape(S5_BLK, LANES), im.reshape(S5_BLK, LANES)

    half = float(S5_R // 2)
    e_re, e_im = lpow(p - half)
    f_re, f_im = lpow(half - p)
    p_re, p_im = outer(e_re, e_im, cre, cim)
    q_re, q_im = outer(f_re, f_im, bb_re, bb_im)
    q_re = jnp.where(low, q_re, 0.0)
    q_im = jnp.where(low, q_im, 0.0)
    mt = (lax.dot_general(p_re, q_re, _NT, precision=lax.Precision.HIGHEST, preferred_element_type=F32)
          - lax.dot_general(p_im, q_im, _NT, precision=lax.Precision.HIGHEST, preferred_element_type=F32))
    row = lax.broadcasted_iota(jnp.int32, (S5_BLK, S5_BLK), 0)
    col = lax.broadcasted_iota(jnp.int32, (S5_BLK, S5_BLK), 1)
    mt = jnp.where(row // S5_GROUP >= col // S5_GROUP, mt, 0.0)
    mt = mt + jnp.where(row == col, dsk_ref[0], 0.0)
    mt_ref[0] = mt.astype(mt_ref.dtype)

    g_re, g_im = lpow(float(S5_R - 1) - p)
    bs_re, bs_im = outer(g_re, g_im, bb_re, bb_im)
    bsre_ref[0] = jnp.where(mine, bs_re, 0.0).astype(bsre_ref.dtype)
    bsim_ref[0] = jnp.where(mine, bs_im, 0.0).astype(bsim_ref.dtype)

    h_re, h_im = lpow(p + 1.0)
    cs_re, cs_im = outer(h_re, h_im, cre, cim)
    ccat_ref[0, :, 0:LANES] = jnp.where(mine, cs_re, 0.0).astype(ccat_ref.dtype)
    ccat_ref[0, :, LANES:2 * LANES] = jnp.where(mine, -cs_im, 0.0).astype(ccat_ref.dtype)

    a_re, a_im = lpow(one * float(S5_R))
    rows8 = lax.broadcasted_iota(jnp.int32, (8, LANES), 0)
    a16_ref[0] = jnp.where(mine, jnp.where(rows8 == 0, a_re, jnp.where(rows8 == 1, a_im, 0.0)), 0.0)


def _s5_prep(are2, aim2, ldt, bre2, bim2, cre2, cim2, dsk_t):
    G = are2.shape[0]
    spec3 = lambda shape: pl.BlockSpec((1,) + shape, lambda g: (g, 0, 0))
    return pl.pallas_call(
        _s5_prep_kernel,
        grid=(G,),
        in_specs=[spec3((1, LANES)), spec3((1, LANES)), spec3((1, 1)),
                  spec3((S5_GROUP, LANES)), spec3((S5_GROUP, LANES)),
                  spec3((S5_GROUP, LANES)), spec3((S5_GROUP, LANES)), spec3((S5_BLK, 1))],
        out_specs=[spec3((S5_BLK, S5_BLK)), spec3((S5_BLK, LANES)), spec3((S5_BLK, LANES)),
                   spec3((S5_BLK, 2 * LANES)), spec3((8, LANES))],
        out_shape=[jax.ShapeDtypeStruct((G, S5_BLK, S5_BLK), BF16),
                   jax.ShapeDtypeStruct((G, S5_BLK, LANES), BF16),
                   jax.ShapeDtypeStruct((G, S5_BLK, LANES), BF16),
                   jax.ShapeDtypeStruct((G, S5_BLK, 2 * LANES), BF16),
                   jax.ShapeDtypeStruct((G, 8, LANES), F32)],
        compiler_params=_cparams(("parallel",)),
        name="s5_prep",
    )(are2, aim2, ldt, bre2, bim2, cre2, cim2, dsk_t)


def _s5_core_kernel(at_ref, mt_ref, bsre_ref, bsim_ref, ccat_ref, a16_ref, yt_ref,
                    zre_ref, zim_ref, xre_ref, xim_ref, yloc_ref, *, nb, nj):
    for b in range(nb):
        rows = pl.ds(b * nj, nj)
        zre_ref[rows, :] = (lax.dot_general(at_ref[b, 0], bsre_ref[0], _TN, preferred_element_type=F32)
                            + lax.dot_general(at_ref[b, 1], bsre_ref[1], _TN, preferred_element_type=F32))
        zim_ref[rows, :] = (lax.dot_general(at_ref[b, 0], bsim_ref[0], _TN, preferred_element_type=F32)
                            + lax.dot_general(at_ref[b, 1], bsim_ref[1], _TN, preferred_element_type=F32))
    a_re = jnp.broadcast_to(a16_ref[0, 0:1, :] + a16_ref[1, 0:1, :], (nb, LANES))
    a_im = jnp.broadcast_to(a16_ref[0, 1:2, :] + a16_ref[1, 1:2, :], (nb, LANES))

    def step(j, carry):
        xr, xi = carry
        idx = pl.ds(j, nb, stride=nj)
        xre_ref[idx, :] = xr
        xim_ref[idx, :] = xi
        zr = zre_ref[idx, :]
        zi = zim_ref[idx, :]
        return a_re * xr - a_im * xi + zr, a_re * xi + a_im * xr + zi

    for b in range(nb):
        for i in range(2):
            yloc_ref[b, i] = jnp.dot(mt_ref[i], at_ref[b, i], preferred_element_type=F32)
    carry = (jnp.zeros((nb, LANES), F32), jnp.zeros((nb, LANES), F32))
    for j in range(nj):
        carry = step(j, carry)
    xcat = [jnp.concatenate([xre_ref[pl.ds(b * nj, nj), :], xim_ref[pl.ds(b * nj, nj), :]], axis=1).astype(BF16)
            for b in range(nb)]
    cross = [[lax.dot_general(ccat_ref[i], xcat[b], _NT, preferred_element_type=F32) for i in range(2)]
             for b in range(nb)]
    for b in range(nb):
        for i in range(2):
            yt_ref[b, i] = (yloc_ref[b, i] + cross[b][i]).astype(yt_ref.dtype)


def _s5_core(at, mt, bsre, bsim, ccat, a16):
    nb, G, _, nj = at.shape
    pair = lambda shape: pl.BlockSpec((2,) + shape, lambda p: (p, 0, 0))
    data = pl.BlockSpec((nb, 2, S5_BLK, nj), lambda p: (0, p, 0, 0))
    return pl.pallas_call(
        functools.partial(_s5_core_kernel, nb=nb, nj=nj),
        grid=(G // 2,),
        in_specs=[data, pair((S5_BLK, S5_BLK)), pair((S5_BLK, LANES)),
                  pair((S5_BLK, LANES)), pair((S5_BLK, 2 * LANES)), pair((8, LANES))],
        out_specs=data,
        out_shape=jax.ShapeDtypeStruct(at.shape, BF16),
        scratch_shapes=[pltpu.VMEM((nb * nj, LANES), F32)] * 4 + [pltpu.VMEM((nb, 2, S5_BLK, nj), F32)],
        compiler_params=_cparams(("parallel",)),
        name="s5_core",
    )(at, mt, bsre, bsim, ccat, a16)


MERGE_CB = 256
MERGE_FRAMES = 2
def _merge0_kernel(x_ref, y5t_ref, g5t_ref, yr_ref, gr_ref, gwt_ref, gb_ref, wo5_ref, wor_ref, o_ref):
    nf, nj = x_ref.shape[1], x_ref.shape[2]
    s5_w = gwt_ref.shape[0]
    frames = range(nf)
    y = [jax.nn.gelu(y5t_ref[0, :, f].astype(F32).reshape(-1, nj)) for f in frames]
    yb = [y[f].astype(BF16) for f in frames]
    r = [yr_ref[0, f].astype(F32) * jax.nn.silu(gr_ref[0, f].astype(F32)) for f in frames]
    blocks = [pl.ds(c0, MERGE_CB) for c0 in range(0, s5_w, MERGE_CB)]
    gates = [[jnp.dot(gwt_ref[cb, :], yb[f], preferred_element_type=F32) + gb_ref[cb, :] for cb in blocks]
             for f in frames]
    out = [x_ref[0, f] + jnp.dot(r[f].astype(BF16), wor_ref[...], preferred_element_type=F32) for f in frames]
    for n, cb in enumerate(blocks):
        for f in frames:
            a = (y[f][n * MERGE_CB:(n + 1) * MERGE_CB] * jax.nn.sigmoid(gates[f][n])
                 * jax.nn.silu(g5t_ref[0, f, cb, :].astype(F32)))
            out[f] = out[f] + lax.dot_general(a.astype(BF16), wo5_ref[cb, :], _TN, preferred_element_type=F32)
    for f in frames:
        o_ref[0, f] = out[f]


def _merge0(x3, y5t, g5t, yr3, gr3, gwt, gb_col, wo5, wor):
    nb, _, nj, d = x3.shape
    G = y5t.shape[1]
    s5_w = g5t.shape[2]
    ret_w = wor.shape[0]
    y5t5 = y5t.reshape(nb, G, S5_R, S5_GROUP, nj)
    strided = lambda wd: pl.BlockSpec((1, MERGE_FRAMES, nj, wd), lambda b, s: (b, s, 0, 0))
    full = lambda a: pl.BlockSpec(a.shape, lambda b, s: (0,) * a.ndim)
    return pl.pallas_call(
        _merge0_kernel,
        grid=(nb, S5_R // MERGE_FRAMES),
        in_specs=[strided(d),
                  pl.BlockSpec((1, G, MERGE_FRAMES, S5_GROUP, nj), lambda b, s: (b, 0, s, 0, 0)),
                  pl.BlockSpec((1, MERGE_FRAMES, s5_w, nj), lambda b, s: (b, s, 0, 0)),
                  strided(ret_w), strided(ret_w), full(gwt), full(gb_col), full(wo5), full(wor)],
        out_specs=strided(d),
        out_shape=jax.ShapeDtypeStruct(x3.shape, F32),
        compiler_params=_cparams(("parallel", "parallel")),
        name="merge0",
    )(x3, y5t5, g5t, yr3, gr3, gwt, gb_col, wo5, wor)


CONV_CARRY = 8
CONV_TB = 8
MM_COLS = 256
LOOP_TRIPS = 4
assert (CONV_K - 1 + S5_R - 1) // S5_R <= CONV_CARRY


def _layer1_kernel(xa_ref, xc_ref, vec_ref, win_ref, cw_ref, wout_ref, o_ref,
                   ubuf0_ref, ubuf1_ref, gate0_ref, gate1_ref, sh_ref, acc_ref, scr_ref,
                   hn_ref, pr_ref, u_ref, od_ref, *, jt, width, nt):
    i = pl.program_id(0)
    rows = S5_R * jt
    d = xa_ref.shape[-1]
    n_shift = sh_ref.shape[1]
    n_strips = acc_ref.shape[0]
    n_proj, n_out = win_ref.shape[0], wout_ref.shape[0]
    n_ab = 2 * width // MM_COLS
    assert n_strips % LOOP_TRIPS == 0 and n_proj % LOOP_TRIPS == 0 and n_out == LOOP_TRIPS

    @pl.when(i == 0)
    def _():
        ubuf1_ref[...] = jnp.zeros(ubuf1_ref.shape, F32)
        gate1_ref[...] = jnp.zeros(gate1_ref.shape, F32)
        u_ref[...] = jnp.zeros(u_ref.shape, u_ref.dtype)

    def conv_strip(ubuf_b, c):
        for dl in range(1, n_shift + 1):
            sh_ref[c, dl - 1] = ubuf_b[c, :, CONV_CARRY - dl:CONV_CARRY - dl + jt, :]
        for s0 in range(0, S5_R, CONV_TB):
            targets = range(s0, s0 + CONV_TB)
            accs = {s: jnp.broadcast_to(cw_ref[c, CONV_K:CONV_K + 1, :], (jt, LANES)) for s in targets}
            for e in range(s0 - (CONV_K - 1), s0 + CONV_TB):
                plane, dl = e % S5_R, -(e // S5_R)
                src = ubuf_b[c, plane, CONV_CARRY:CONV_CARRY + jt, :] if dl == 0 else sh_ref[c, dl - 1, plane]
                for s in targets:
                    kk = e + (CONV_K - 1) - s
                    if 0 <= kk < CONV_K:
                        accs[s] = accs[s] + src * cw_ref[c, kk:kk + 1, :]
            for s in targets:
                acc_ref[c, s * jt:(s + 1) * jt, :] = accs[s]

    def step(ubuf_a, gate_a, ubuf_b, gate_b):
        hn_ref[...] = _rms(xa_ref[0].reshape(rows, d), vec_ref[0:1, :]).astype(BF16)

        def body(it, carry):
            for k in range(n_strips // LOOP_TRIPS):
                conv_strip(ubuf_b, it * (n_strips // LOOP_TRIPS) + k)
            for k in range(n_proj // LOOP_TRIPS):
                n = it * (n_proj // LOOP_TRIPS) + k
                pr_ref[n] = jnp.dot(hn_ref[...], win_ref[n], preferred_element_type=F32)
            od_ref[it] = jnp.dot(u_ref[...], wout_ref[it], preferred_element_type=F32)
            return carry

        for it in range(LOOP_TRIPS):
            @pl.when(i >= 0)
            def _(it=it):
                body(it, 0)

        x1 = xc_ref[0].reshape(rows, d) + jnp.concatenate([od_ref[n] for n in range(n_out)], axis=1)
        _from_frames(_rms(x1, vec_ref[3:4, :]), scr_ref, o_ref)

        acc = jnp.concatenate([acc_ref[c] for c in range(n_strips)], axis=1)
        mu = jnp.mean(acc, axis=-1, keepdims=True)
        var = jnp.mean(jnp.square(acc - mu), axis=-1, keepdims=True)
        uf = (acc - mu) * lax.rsqrt(var + EPS) * vec_ref[1:2, :] + vec_ref[2:3, :]
        gate = jnp.concatenate([gate_b[n] for n in range(gate_b.shape[0])], axis=1)
        u_ref[...] = (jax.nn.silu(uf) * jax.nn.silu(gate)).astype(u_ref.dtype)

        first = (i % nt) == 0
        for n in range(n_proj - n_ab):
            gate_a[n] = pr_ref[n_ab + n]
        for n in range(n_ab // 2):
            glu = pr_ref[n] * jax.nn.sigmoid(pr_ref[n_ab // 2 + n])
            for k in range(MM_COLS // LANES):
                c = n * (MM_COLS // LANES) + k
                ubuf_a[c, :, CONV_CARRY:CONV_CARRY + jt, :] = glu[:, k * LANES:(k + 1) * LANES].reshape(S5_R, jt, LANES)
        for c in range(n_strips):
            hist = ubuf_b[c, :, jt:jt + CONV_CARRY, :]
            ubuf_a[c, :, 0:CONV_CARRY, :] = jnp.where(first, jnp.zeros_like(hist), hist)

    @pl.when(i % 2 == 0)
    def _():
        step(ubuf0_ref, gate0_ref, ubuf1_ref, gate1_ref)

    @pl.when(i % 2 == 1)
    def _():
        step(ubuf1_ref, gate1_ref, ubuf0_ref, gate0_ref)


def _layer1(x3, g, win, cw, cb, lg, lb, wout, fg, tm):
    nb, _, nj, d = x3.shape
    width = wout.shape[0]
    jt = tm // S5_R
    nt = nj // jt
    n_tiles = nb * nt
    n_strips = width // LANES
    n_shift = (CONV_K - 1 + S5_R - 1) // S5_R
    win3 = win.reshape(d, 3 * width // MM_COLS, MM_COLS).transpose(1, 0, 2)
    wout3 = wout.reshape(width, d // MM_COLS, MM_COLS).transpose(1, 0, 2)
    cw3 = jnp.concatenate([cw, cb], axis=0).reshape(CONV_K + 1, n_strips, LANES).transpose(1, 0, 2)
    vec = jnp.concatenate([g, lg, lb, fg, jnp.zeros((4, d), F32)], axis=0)
    full = lambda a: pl.BlockSpec(a.shape, lambda i: (0,) * a.ndim, pipeline_mode=pl.Buffered(1))
    tile_a = lambda i: jnp.minimum(i, n_tiles - 1)
    tile_c = lambda i: jnp.clip(i - 2, 0, n_tiles - 1)
    plane_blk = lambda tile: pl.BlockSpec((1, S5_R, jt, d), lambda i: (tile(i) // nt, 0, tile(i) % nt, 0))
    ubuf = pltpu.VMEM((n_strips, S5_R, CONV_CARRY + jt, LANES), F32)
    gate = pltpu.VMEM((width // MM_COLS, tm, MM_COLS), F32)
    return pl.pallas_call(
        functools.partial(_layer1_kernel, jt=jt, width=width, nt=nt),
        grid=(n_tiles + 2,),
        in_specs=[plane_blk(tile_a), plane_blk(tile_c),
                  full(vec), full(win3), full(cw3), full(wout3)],
        out_specs=pl.BlockSpec((tm, d), lambda i: (tile_c(i), 0)),
        out_shape=jax.ShapeDtypeStruct((nb * nj * S5_R, d), F32),
        scratch_shapes=[ubuf, ubuf, gate, gate,
                        pltpu.VMEM((n_strips, n_shift, S5_R, jt, LANES), F32),
                        pltpu.VMEM((n_strips, tm, LANES), F32),
                        _frame_scratch(d, tm),
                        pltpu.VMEM((tm, d), BF16),
                        pltpu.VMEM((3 * width // MM_COLS, tm, MM_COLS), F32),
                        pltpu.VMEM((tm, width), BF16),
                        pltpu.VMEM((d // MM_COLS, tm, MM_COLS), F32)],
        compiler_params=_cparams(("arbitrary",)),
        name="layer1",
    )(x3, x3, vec, win3, cw3, wout3)


def _pick_tile(n, pref):
    t = min(n, pref)
    while n % t:
        t //= 2
    return t


def kernel(x, norm_g, final_g, w_in_ab, s5_a_re, s5_a_im, s5_log_dt, s5_b_re, s5_b_im, s5_c_re, s5_c_im,
           s5_d, s5_glu_w, s5_glu_b, w_out_ab, w_in_c, conv_w, conv_b, conv_ln_g, conv_ln_b, w_out_c):
    nb, seq, d = x.shape
    rows = nb * seq
    s5_w = s5_glu_w.shape[-1]
    G = s5_w // S5_GROUP
    ret_qk = RET_HEADS * RET_DK
    ret_w = RET_HEADS * RET_DV
    nj = seq // S5_R
    x2 = x.reshape(rows, d)

    w_in = w_in_ab[0].astype(BF16)
    tm = _pick_tile(seq, 512)

    dup = lambda a: jnp.concatenate([a, a], axis=-1)
    are2 = dup(s5_a_re[0])[:, None, :]
    aim2 = dup(s5_a_im[0])[:, None, :]
    ldt = s5_log_dt[0][:, None, None]
    bre2 = dup(jnp.swapaxes(s5_b_re[0], 1, 2))
    bim2 = dup(jnp.swapaxes(s5_b_im[0], 1, 2))
    cre2 = dup(s5_c_re[0])
    cim2 = dup(s5_c_im[0])
    dsk_t = jnp.tile(s5_d[0].reshape(G, 1, S5_GROUP), (1, S5_R, 1)).reshape(G, S5_BLK, 1)
    mt, bsre, bsim, ccat, a16 = _s5_prep(are2, aim2, ldt, bre2, bim2, cre2, cim2, dsk_t)

    pos = jnp.arange(seq, dtype=F32)[:, None]
    half = RET_DK // 2
    freqs = ROPE_BASE ** (-jnp.arange(half, dtype=F32) / half)
    ang = pos * freqs[None, :]
    cos_t = jnp.tile(jnp.concatenate([jnp.cos(ang), jnp.cos(ang)], axis=-1), (1, LANES // RET_DK))
    sin_t = jnp.tile(jnp.concatenate([-jnp.sin(ang), jnp.sin(ang)], axis=-1), (1, LANES // RET_DK))
    log_g = jnp.log(1.0 - 2.0 ** (-5.0 - jnp.arange(RET_HEADS, dtype=F32)))
    idx = jnp.arange(RET_BLK, dtype=F32)
    chunk_of = jnp.arange(RET_BLK) // CHUNK
    seen = chunk_of[:, None] >= chunk_of[None, :]
    dm = jnp.where(seen[None], jnp.exp(log_g[:, None, None] * jnp.abs(idx[:, None] - idx[None, :])[None]), 0.0)
    qdec = jnp.exp(log_g[:, None] * (idx + 1.0)[None, :])[:, :, None]
    kdec = jnp.exp(log_g[:, None] * (RET_BLK - 1 - idx)[None, :])[:, :, None]
    cdec = jnp.broadcast_to(jnp.exp(log_g * RET_BLK)[:, None, None], (RET_HEADS, 1, RET_DV))
    yr3, gr3, hn3, x3 = _proj0_ret(x2, norm_g[0:1], w_in[:, 2 * s5_w:], cos_t, sin_t, dm, qdec, kdec, cdec,
                                   nb, seq, ret_qk, ret_w, tm)
    at, g5t = _proj_u(hn3, w_in[:, :2 * s5_w].T, s5_w, 4)
    y5t = _s5_core(at, mt, bsre, bsim, ccat, a16)

    wo = w_out_ab[0].astype(BF16)
    x3 = _merge0(x3, y5t, g5t, yr3, gr3, s5_glu_w[0].T.astype(BF16), s5_glu_b[0][:, None],
                 wo[:s5_w], wo[s5_w:])

    out = _layer1(x3, norm_g[1:2], w_in_c[0].astype(BF16), conv_w[0], conv_b[0:1], conv_ln_g[0:1],
                  conv_ln_b[0:1], w_out_c[0].astype(BF16), final_g[None, :], tm)
    return out.reshape(nb, seq, d)
```

```python
import functools
import math

import jax
import jax.numpy as jnp
from jax import lax
from jax.experimental import pallas as pl
from jax.experimental.pallas import tpu as pltpu

F32 = jnp.float32
BF16 = jnp.bfloat16

EPS = 1e-6
CHUNK = 64
S5_GROUP = 16
S5_STATE = 64
S5_R = 16
S5_BLK = S5_R * S5_GROUP
RET_HEADS = 8
RET_DK = 64
RET_DV = 128
RET_BLK = 256
ROPE_BASE = 10000.0
CONV_K = 31
CONV_HALO = 32

LANES = 128
VMEM_LIMIT_BYTES = 56 * 1024 * 1024

_NT = (((1,), (1,)), ((), ()))
_TN = (((0,), (0,)), ((), ()))


def _cparams(sem):
    return pltpu.CompilerParams(dimension_semantics=sem, vmem_limit_bytes=VMEM_LIMIT_BYTES)


def _rms(x, g):
    return x * lax.rsqrt(jnp.mean(x * x, axis=-1, keepdims=True) + EPS) * g


FRAME_PITCH = 24
assert FRAME_PITCH >= S5_R and FRAME_PITCH % 8 == 0


def _frame_scratch(width, rows):
    return pltpu.VMEM((width // LANES, rows // S5_R * FRAME_PITCH, LANES), F32)


def _to_frames(val, scr_ref, dst_ref):
    nc = scr_ref.shape[0]
    jt = scr_ref.shape[1] // FRAME_PITCH
    for c in range(nc):
        for j in range(jt):
            scr_ref[c, j * FRAME_PITCH:j * FRAME_PITCH + S5_R, :] = val[j * S5_R:(j + 1) * S5_R, c * LANES:(c + 1) * LANES]
    for s in range(S5_R):
        dst_ref[0, s] = jnp.concatenate(
            [scr_ref[c, pl.ds(s, jt, stride=FRAME_PITCH), :] for c in range(nc)], axis=1).astype(dst_ref.dtype)


def _from_frames(val, scr_ref, dst_ref):
    nc = scr_ref.shape[0]
    jt = scr_ref.shape[1] // FRAME_PITCH
    for s in range(S5_R):
        for c in range(nc):
            scr_ref[c, pl.ds(s, jt, stride=FRAME_PITCH), :] = val[s * jt:(s + 1) * jt, c * LANES:(c + 1) * LANES]
    for j in range(jt):
        dst_ref[j * S5_R:(j + 1) * S5_R, :] = jnp.concatenate(
            [scr_ref[c, j * FRAME_PITCH:j * FRAME_PITCH + S5_R, :] for c in range(nc)], axis=1)


def _proj0_ret_kernel(x_ref, g_ref, w_ref, cos_ref, sin_ref, dm_ref, qdec_ref, kdec_ref, cdec_ref,
                      yr3_ref, gr3_ref, hn3_ref, x3_ref,
                      s_ref, qs_ref, ks_ref, vs_ref, o_ref, scr_ref, *, n_blocks):
    @pl.when(pl.program_id(1) == 0)
    def _():
        s_ref[...] = jnp.zeros(s_ref.shape, s_ref.dtype)

    qk_w, v_w = qs_ref.shape[-1], vs_ref.shape[-1]
    x = x_ref[...]
    hn32 = _rms(x, g_ref[...])
    hn = hn32.astype(BF16)
    q = jnp.dot(hn, w_ref[:, 0:qk_w], preferred_element_type=F32)
    k = jnp.dot(hn, w_ref[:, qk_w:2 * qk_w], preferred_element_type=F32)
    v = jnp.dot(hn, w_ref[:, 2 * qk_w:2 * qk_w + v_w], preferred_element_type=F32)
    gr = jnp.dot(hn, w_ref[:, 2 * qk_w + v_w:], preferred_element_type=F32)

    cos = cos_ref[...]
    sin = sin_ref[...]
    lane = lax.broadcasted_iota(jnp.int32, cos.shape, 1)
    first = (lane % RET_DK) < (RET_DK // 2)

    def rope(t):
        out = []
        for p in range(qk_w // LANES):
            tp = t[:, p * LANES:(p + 1) * LANES]
            partner = jnp.where(first, pltpu.roll(tp, LANES - RET_DK // 2, 1), pltpu.roll(tp, RET_DK // 2, 1))
            out.append(tp * cos + partner * sin)
        return jnp.concatenate(out, axis=1)

    qs_ref[...] = (rope(q) * (RET_DK ** -0.5)).astype(BF16)
    ks_ref[...] = rope(k).astype(BF16)
    vs_ref[...] = v.astype(BF16)
    _to_frames(hn32, scr_ref, hn3_ref)
    _to_frames(x, scr_ref, x3_ref)
    _to_frames(gr, scr_ref, gr3_ref)

    lane_p = lax.broadcasted_iota(jnp.int32, (RET_BLK, LANES), 1)
    heads = range(RET_HEADS)
    for c in range(n_blocks):
        rows = pl.ds(c * RET_BLK, RET_BLK)
        qp = [qs_ref[rows, pl.ds((h // 2) * LANES, LANES)] for h in heads]
        km = [jnp.where((lane_p // RET_DK) == (h % 2), ks_ref[rows, pl.ds((h // 2) * LANES, LANES)],
                        jnp.zeros((RET_BLK, LANES), BF16)) for h in heads]
        vh = [vs_ref[rows, pl.ds(h * RET_DV, RET_DV)] for h in heads]
        sc = [lax.dot_general(qp[h], km[h], _NT, preferred_element_type=F32) for h in heads]
        s_prev = [s_ref[h] for h in heads]
        cross = [jnp.dot(qp[h], s_prev[h].astype(BF16), preferred_element_type=F32) for h in heads]
        p = [(sc[h] * dm_ref[h]).astype(BF16) for h in heads]
        o = [jnp.dot(p[h], vh[h], preferred_element_type=F32) + qdec_ref[h] * cross[h] for h in heads]
        vd = [(vh[h].astype(F32) * kdec_ref[h]).astype(BF16) for h in heads]
        kv = [lax.dot_general(km[h], vd[h], _TN, preferred_element_type=F32) for h in heads]
        for h in heads:
            s_ref[h] = cdec_ref[h] * s_prev[h] + kv[h]
        mu = [jnp.mean(o[h], axis=-1, keepdims=True) for h in heads]
        var = [jnp.mean(jnp.square(o[h] - mu[h]), axis=-1, keepdims=True) for h in heads]
        for h in heads:
            res = (o[h] - mu[h]) * lax.rsqrt(var[h] + EPS)
            for j in range(RET_BLK // S5_R):
                j0 = (c * (RET_BLK // S5_R) + j) * FRAME_PITCH
                o_ref[h, j0:j0 + S5_R, :] = res[j * S5_R:(j + 1) * S5_R]
    jt = yr3_ref.shape[2]
    for s in range(S5_R):
        yr3_ref[0, s] = jnp.concatenate(
            [o_ref[h, pl.ds(s, jt, stride=FRAME_PITCH), :] for h in range(RET_HEADS)], axis=1).astype(yr3_ref.dtype)


def _proj0_ret(x2, g, w, cos, sin, dm, qdec, kdec, cdec, nb, seq, qk_w, v_w, tm):
    rows, d = x2.shape
    assert tm % RET_BLK == 0
    assert d == v_w, "the strided-read scratch is shared between hn, x and the retention gate"
    nt = seq // tm
    jt = tm // S5_R
    nj = seq // S5_R
    full = lambda a: pl.BlockSpec(a.shape, lambda b, t: (0,) * a.ndim)
    pos = pl.BlockSpec((tm, LANES), lambda b, t: (t, 0))
    fm = lambda wd: pl.BlockSpec((1, S5_R, jt, wd), lambda b, t: (b, 0, t, 0))
    fm_shape = lambda wd, dt: jax.ShapeDtypeStruct((nb, S5_R, nj, wd), dt)
    return pl.pallas_call(
        functools.partial(_proj0_ret_kernel, n_blocks=tm // RET_BLK),
        grid=(nb, nt),
        in_specs=[pl.BlockSpec((tm, d), lambda b, t: (b * nt + t, 0)), full(g), full(w), pos, pos,
                  full(dm), full(qdec), full(kdec), full(cdec)],
        out_specs=[fm(v_w), fm(v_w), fm(d), fm(d)],
        out_shape=[fm_shape(v_w, BF16), fm_shape(v_w, BF16), fm_shape(d, BF16), fm_shape(d, F32)],
        scratch_shapes=[pltpu.VMEM((RET_HEADS, LANES, RET_DV), F32),
                        pltpu.VMEM((tm, qk_w), BF16), pltpu.VMEM((tm, qk_w), BF16), pltpu.VMEM((tm, v_w), BF16),
                        _frame_scratch(v_w, tm), _frame_scratch(d, tm)],
        compiler_params=_cparams(("parallel", "arbitrary")),
        name="proj0_ret",
    )(x2, g, w, cos, sin, dm, qdec, kdec, cdec)


def _proj_u_kernel(hn3_ref, wt_ref, at_ref, g5t_ref, *, sg, s5_w):
    nj = hn3_ref.shape[2]
    for sl in range(sg):
        ut = lax.dot_general(wt_ref[...], hn3_ref[0, sl], _NT, preferred_element_type=F32)
        at_ref[0, :, sl * S5_GROUP:(sl + 1) * S5_GROUP, :] = (
            ut[:s5_w].reshape(s5_w // S5_GROUP, S5_GROUP, nj).astype(at_ref.dtype))
        g5t_ref[0, sl] = ut[s5_w:].astype(g5t_ref.dtype)


def _proj_u(hn3, wt, s5_w, sg):
    nb, _, nj, d = hn3.shape
    G = s5_w // S5_GROUP
    return pl.pallas_call(
        functools.partial(_proj_u_kernel, sg=sg, s5_w=s5_w),
        grid=(nb, S5_R // sg),
        in_specs=[
            pl.BlockSpec((1, sg, nj, d), lambda b, s: (b, s, 0, 0)),
            pl.BlockSpec(wt.shape, lambda b, s: (0, 0)),
        ],
        out_specs=[
            pl.BlockSpec((1, G, sg * S5_GROUP, nj), lambda b, s: (b, 0, s, 0)),
            pl.BlockSpec((1, sg, s5_w, nj), lambda b, s: (b, s, 0, 0)),
        ],
        out_shape=[jax.ShapeDtypeStruct((nb, G, S5_BLK, nj), BF16),
                   jax.ShapeDtypeStruct((nb, S5_R, s5_w, nj), BF16)],
        compiler_params=_cparams(("parallel", "parallel")),
        name="proj_u",
    )(hn3, wt)


def _s5_prep_kernel(are_ref, aim_ref, ldt_ref, bre_ref, bim_ref, cre_ref, cim_ref, dsk_ref,
                    mt_ref, bsre_ref, bsim_ref, ccat_ref, a16_ref):
    g = pl.program_id(0)
    are = are_ref[0]
    aim = aim_ref[0]
    dt = jnp.exp(ldt_ref[0])
    bre, bim = bre_ref[0], bim_ref[0]
    cre, cim = cre_ref[0], cim_ref[0]

    def lpow(e):
        mag = jnp.exp(are * dt * e)
        ang = aim * dt * e
        return mag * jnp.cos(ang), mag * jnp.sin(ang)

    one = jnp.ones((1, 1), F32)
    lb_re, lb_im = lpow(one)
    nr, ni = lb_re - 1.0, lb_im
    den = are * are + aim * aim
    coef_re = (nr * are + ni * aim) / den
    coef_im = (ni * are - nr * aim) / den
    bb_re = coef_re * bre - coef_im * bim
    bb_im = coef_re * bim + coef_im * bre

    p = lax.broadcasted_iota(jnp.int32, (S5_R, 1), 0).astype(F32)
    lane = lax.broadcasted_iota(jnp.int32, (1, LANES), 1)
    mine = (lane // S5_STATE) == (g % 2)
    low = lane < S5_STATE

    def outer(t_re, t_im, m_re, m_im):
        re = t_re[:, None, :] * m_re[None, :, :] - t_im[:, None, :] * m_im[None, :, :]
        im = t_re[:, None, :] * m_im[None, :, :] + t_im[:, None, :] * m_re[None, :, :]
        return re.reshape(S5_BLK, LANES), im.reshape(S5_BLK, LANES)

    half = float(S5_R // 2)
    e_re, e_im = lpow(p - half)
    f_re, f_im = lpow(half - p)
    p_re, p_im = outer(e_re, e_im, cre, cim)
    q_re, q_im = outer(f_re, f_im, bb_re, bb_im)
    q_re = jnp.where(low, q_re, 0.0)
    q_im = jnp.where(low, q_im, 0.0)
    mt = (lax.dot_general(p_re, q_re, _NT, precision=lax.Precision.HIGHEST, preferred_element_type=F32)
          - lax.dot_general(p_im, q_im, _NT, precision=lax.Precision.HIGHEST, preferred_element_type=F32))
    row = lax.broadcasted_iota(jnp.int32, (S5_BLK, S5_BLK), 0)
    col = lax.broadcasted_iota(jnp.int32, (S5_BLK, S5_BLK), 1)
    mt = jnp.where(row // S5_GROUP >= col // S5_GROUP, mt, 0.0)
    mt = mt + jnp.where(row == col, dsk_ref[0], 0.0)
    mt_ref[0] = mt.astype(mt_ref.dtype)

    g_re, g_im = lpow(float(S5_R - 1) - p)
    bs_re, bs_im = outer(g_re, g_im, bb_re, bb_im)
    bsre_ref[0] = jnp.where(mine, bs_re, 0.0).astype(bsre_ref.dtype)
    bsim_ref[0] = jnp.where(mine, bs_im, 0.0).astype(bsim_ref.dtype)

    h_re, h_im = lpow(p + 1.0)
    cs_re, cs_im = outer(h_re, h_im, cre, cim)
    ccat_ref[0, :, 0:LANES] = jnp.where(mine, cs_re, 0.0).astype(ccat_ref.dtype)
    ccat_ref[0, :, LANES:2 * LANES] = jnp.where(mine, -cs_im, 0.0).astype(ccat_ref.dtype)

    a_re, a_im = lpow(one * float(S5_R))
    rows8 = lax.broadcasted_iota(jnp.int32, (8, LANES), 0)
    a16_ref[0] = jnp.where(mine, jnp.where(rows8 == 0, a_re, jnp.where(rows8 == 1, a_im, 0.0)), 0.0)


def _s5_prep(are2, aim2, ldt, bre2, bim2, cre2, cim2, dsk_t):
    G = are2.shape[0]
    spec3 = lambda shape: pl.BlockSpec((1,) + shape, lambda g: (g, 0, 0))
    return pl.pallas_call(
        _s5_prep_kernel,
        grid=(G,),
        in_specs=[spec3((1, LANES)), spec3((1, LANES)), spec3((1, 1)),
                  spec3((S5_GROUP, LANES)), spec3((S5_GROUP, LANES)),
                  spec3((S5_GROUP, LANES)), spec3((S5_GROUP, LANES)), spec3((S5_BLK, 1))],
        out_specs=[spec3((S5_BLK, S5_BLK)), spec3((S5_BLK, LANES)), spec3((S5_BLK, LANES)),
                   spec3((S5_BLK, 2 * LANES)), spec3((8, LANES))],
        out_shape=[jax.ShapeDtypeStruct((G, S5_BLK, S5_BLK), BF16),
                   jax.ShapeDtypeStruct((G, S5_BLK, LANES), BF16),
                   jax.ShapeDtypeStruct((G, S5_BLK, LANES), BF16),
                   jax.ShapeDtypeStruct((G, S5_BLK, 2 * LANES), BF16),
                   jax.ShapeDtypeStruct((G, 8, LANES), F32)],
        compiler_params=_cparams(("parallel",)),
        name="s5_prep",
    )(are2, aim2, ldt, bre2, bim2, cre2, cim2, dsk_t)


def _s5_core_kernel(at_ref, mt_ref, bsre_ref, bsim_ref, ccat_ref, a16_ref, yt_ref,
                    zre_ref, zim_ref, xre_ref, xim_ref, yloc_ref, *, nb, nj):
    for b in range(nb):
        rows = pl.ds(b * nj, nj)
        zre_ref[rows, :] = (lax.dot_general(at_ref[b, 0], bsre_ref[0], _TN, preferred_element_type=F32)
                            + lax.dot_general(at_ref[b, 1], bsre_ref[1], _TN, preferred_element_type=F32))
        zim_ref[rows, :] = (lax.dot_general(at_ref[b, 0], bsim_ref[0], _TN, preferred_element_type=F32)
                            + lax.dot_general(at_ref[b, 1], bsim_ref[1], _TN, preferred_element_type=F32))
    a_re = jnp.broadcast_to(a16_ref[0, 0:1, :] + a16_ref[1, 0:1, :], (nb, LANES))
    a_im = jnp.broadcast_to(a16_ref[0, 1:2, :] + a16_ref[1, 1:2, :], (nb, LANES))

    def step(j, carry):
        xr, xi = carry
        idx = pl.ds(j, nb, stride=nj)
        xre_ref[idx, :] = xr
        xim_ref[idx, :] = xi
        zr = zre_ref[idx, :]
        zi = zim_ref[idx, :]
        return a_re * xr - a_im * xi + zr, a_re * xi + a_im * xr + zi

    for b in range(nb):
        for i in range(2):
            yloc_ref[b, i] = jnp.dot(mt_ref[i], at_ref[b, i], preferred_element_type=F32)
    carry = (jnp.zeros((nb, LANES), F32), jnp.zeros((nb, LANES), F32))
    for j in range(nj):
        carry = step(j, carry)
    xcat = [jnp.concatenate([xre_ref[pl.ds(b * nj, nj), :], xim_ref[pl.ds(b * nj, nj), :]], axis=1).astype(BF16)
            for b in range(nb)]
    cross = [[lax.dot_general(ccat_ref[i], xcat[b], _NT, preferred_element_type=F32) for i in range(2)]
             for b in range(nb)]
    for b in range(nb):
        for i in range(2):
            yt_ref[b, i] = (yloc_ref[b, i] + cross[b][i]).astype(yt_ref.dtype)


def _s5_core(at, mt, bsre, bsim, ccat, a16):
    nb, G, _, nj = at.shape
    pair = lambda shape: pl.BlockSpec((2,) + shape, lambda p: (p, 0, 0))
    data = pl.BlockSpec((nb, 2, S5_BLK, nj), lambda p: (0, p, 0, 0))
    return pl.pallas_call(
        functools.partial(_s5_core_kernel, nb=nb, nj=nj),
        grid=(G // 2,),
        in_specs=[data, pair((S5_BLK, S5_BLK)), pair((S5_BLK, LANES)),
                  pair((S5_BLK, LANES)), pair((S5_BLK, 2 * LANES)), pair((8, LANES))],
        out_specs=data,
        out_shape=jax.ShapeDtypeStruct(at.shape, BF16),
        scratch_shapes=[pltpu.VMEM((nb * nj, LANES), F32)] * 4 + [pltpu.VMEM((nb, 2, S5_BLK, nj), F32)],
        compiler_params=_cparams(("parallel",)),
        name="s5_core",
    )(at, mt, bsre, bsim, ccat, a16)


MERGE_CB = 256
MERGE_FRAMES = 4
def _merge0_kernel(x_ref, y5t_ref, g5t_ref, yr_ref, gr_ref, gwt_ref, gb_ref, wo5_ref, wor_ref, o_ref):
    nf, nj = x_ref.shape[1], x_ref.shape[2]
    s5_w = gwt_ref.shape[0]
    frames = range(nf)
    y = [jax.nn.gelu(y5t_ref[0, :, f].astype(F32).reshape(-1, nj)) for f in frames]
    yb = [y[f].astype(BF16) for f in frames]
    r = [yr_ref[0, f].astype(F32) * jax.nn.silu(gr_ref[0, f].astype(F32)) for f in frames]
    blocks = [pl.ds(c0, MERGE_CB) for c0 in range(0, s5_w, MERGE_CB)]
    gates = [[jnp.dot(gwt_ref[cb, :], yb[f], preferred_element_type=F32) + gb_ref[cb, :] for cb in blocks]
             for f in frames]
    out = [x_ref[0, f] + jnp.dot(r[f].astype(BF16), wor_ref[...], preferred_element_type=F32) for f in frames]
    for n, cb in enumerate(blocks):
        for f in frames:
            a = (y[f][n * MERGE_CB:(n + 1) * MERGE_CB] * jax.nn.sigmoid(gates[f][n])
                 * jax.nn.silu(g5t_ref[0, f, cb, :].astype(F32)))
            out[f] = out[f] + lax.dot_general(a.astype(BF16), wo5_ref[cb, :], _TN, preferred_element_type=F32)
    for f in frames:
        o_ref[0, f] = out[f]


def _merge0(x3, y5t, g5t, yr3, gr3, gwt, gb_col, wo5, wor):
    nb, _, nj, d = x3.shape
    G = y5t.shape[1]
    s5_w = g5t.shape[2]
    ret_w = wor.shape[0]
    y5t5 = y5t.reshape(nb, G, S5_R, S5_GROUP, nj)
    strided = lambda wd: pl.BlockSpec((1, MERGE_FRAMES, nj, wd), lambda b, s: (b, s, 0, 0))
    full = lambda a: pl.BlockSpec(a.shape, lambda b, s: (0,) * a.ndim)
    return pl.pallas_call(
        _merge0_kernel,
        grid=(nb, S5_R // MERGE_FRAMES),
        in_specs=[strided(d),
                  pl.BlockSpec((1, G, MERGE_FRAMES, S5_GROUP, nj), lambda b, s: (b, 0, s, 0, 0)),
                  pl.BlockSpec((1, MERGE_FRAMES, s5_w, nj), lambda b, s: (b, s, 0, 0)),
                  strided(ret_w), strided(ret_w), full(gwt), full(gb_col), full(wo5), full(wor)],
        out_specs=strided(d),
        out_shape=jax.ShapeDtypeStruct(x3.shape, F32),
        compiler_params=_cparams(("parallel", "parallel")),
        name="merge0",
    )(x3, y5t5, g5t, yr3, gr3, gwt, gb_col, wo5, wor)


CONV_CARRY = 8
CONV_TB = 8
MM_COLS = 256
LOOP_TRIPS = 4
assert (CONV_K - 1 + S5_R - 1) // S5_R <= CONV_CARRY


def _layer1_kernel(xa_ref, xc_ref, vec_ref, win_ref, cw_ref, wout_ref, o_ref,
                   ubuf0_ref, ubuf1_ref, gate0_ref, gate1_ref, sh_ref, acc_ref, scr_ref,
                   hn_ref, pr_ref, u_ref, od_ref, *, jt, width, nt):
    i = pl.program_id(0)
    rows = S5_R * jt
    d = xa_ref.shape[-1]
    n_shift = sh_ref.shape[1]
    n_strips = acc_ref.shape[0]
    n_proj, n_out = win_ref.shape[0], wout_ref.shape[0]
    n_ab = 2 * width // MM_COLS
    assert n_strips % LOOP_TRIPS == 0 and n_proj % LOOP_TRIPS == 0 and n_out == LOOP_TRIPS

    @pl.when(i == 0)
    def _():
        ubuf1_ref[...] = jnp.zeros(ubuf1_ref.shape, F32)
        gate1_ref[...] = jnp.zeros(gate1_ref.shape, F32)
        u_ref[...] = jnp.zeros(u_ref.shape, u_ref.dtype)

    def conv_strip(ubuf_b, c):
        for dl in range(1, n_shift + 1):
            sh_ref[c, dl - 1] = ubuf_b[c, :, CONV_CARRY - dl:CONV_CARRY - dl + jt, :]
        for s0 in range(0, S5_R, CONV_TB):
            targets = range(s0, s0 + CONV_TB)
            accs = {s: jnp.broadcast_to(cw_ref[c, CONV_K:CONV_K + 1, :], (jt, LANES)) for s in targets}
            for e in range(s0 - (CONV_K - 1), s0 + CONV_TB):
                plane, dl = e % S5_R, -(e // S5_R)
                src = ubuf_b[c, plane, CONV_CARRY:CONV_CARRY + jt, :] if dl == 0 else sh_ref[c, dl - 1, plane]
                for s in targets:
                    kk = e + (CONV_K - 1) - s
                    if 0 <= kk < CONV_K:
                        accs[s] = accs[s] + src * cw_ref[c, kk:kk + 1, :]
            for s in targets:
                acc_ref[c, s * jt:(s + 1) * jt, :] = accs[s]

    def step(ubuf_a, gate_a, ubuf_b, gate_b):
        hn_ref[...] = _rms(xa_ref[0].reshape(rows, d), vec_ref[0:1, :]).astype(BF16)

        def body(it, carry):
            for k in range(n_strips // LOOP_TRIPS):
                conv_strip(ubuf_b, it * (n_strips // LOOP_TRIPS) + k)
            for k in range(n_proj // LOOP_TRIPS):
                n = it * (n_proj // LOOP_TRIPS) + k
                pr_ref[n] = jnp.dot(hn_ref[...], win_ref[n], preferred_element_type=F32)
            od_ref[it] = jnp.dot(u_ref[...], wout_ref[it], preferred_element_type=F32)
            return carry

        for it in range(LOOP_TRIPS):
            @pl.when(i >= 0)
            def _(it=it):
                body(it, 0)

        x1 = xc_ref[0].reshape(rows, d) + jnp.concatenate([od_ref[n] for n in range(n_out)], axis=1)
        _from_frames(_rms(x1, vec_ref[3:4, :]), scr_ref, o_ref)

        acc = jnp.concatenate([acc_ref[c] for c in range(n_strips)], axis=1)
        mu = jnp.mean(acc, axis=-1, keepdims=True)
        var = jnp.mean(jnp.square(acc - mu), axis=-1, keepdims=True)
        uf = (acc - mu) * lax.rsqrt(var + EPS) * vec_ref[1:2, :] + vec_ref[2:3, :]
        gate = jnp.concatenate([gate_b[n] for n in range(gate_b.shape[0])], axis=1)
        u_ref[...] = (jax.nn.silu(uf) * jax.nn.silu(gate)).astype(u_ref.dtype)

        first = (i % nt) == 0
        for n in range(n_proj - n_ab):
            gate_a[n] = pr_ref[n_ab + n]
        for n in range(n_ab // 2):
            glu = pr_ref[n] * jax.nn.sigmoid(pr_ref[n_ab // 2 + n])
            for k in range(MM_COLS // LANES):
                c = n * (MM_COLS // LANES) + k
                ubuf_a[c, :, CONV_CARRY:CONV_CARRY + jt, :] = glu[:, k * LANES:(k + 1) * LANES].reshape(S5_R, jt, LANES)
        for c in range(n_strips):
            hist = ubuf_b[c, :, jt:jt + CONV_CARRY, :]
            ubuf_a[c, :, 0:CONV_CARRY, :] = jnp.where(first, jnp.zeros_like(hist), hist)

    @pl.when(i % 2 == 0)
    def _():
        step(ubuf0_ref, gate0_ref, ubuf1_ref, gate1_ref)

    @pl.when(i % 2 == 1)
    def _():
        step(ubuf1_ref, gate1_ref, ubuf0_ref, gate0_ref)


def _layer1(x3, g, win, cw, cb, lg, lb, wout, fg, tm):
    nb, _, nj, d = x3.shape
    width = wout.shape[0]
    jt = tm // S5_R
    nt = nj // jt
    n_tiles = nb * nt
    n_strips = width // LANES
    n_shift = (CONV_K - 1 + S5_R - 1) // S5_R
    win3 = win.reshape(d, 3 * width // MM_COLS, MM_COLS).transpose(1, 0, 2)
    wout3 = wout.reshape(width, d // MM_COLS, MM_COLS).transpose(1, 0, 2)
    cw3 = jnp.concatenate([cw, cb], axis=0).reshape(CONV_K + 1, n_strips, LANES).transpose(1, 0, 2)
    vec = jnp.concatenate([g, lg, lb, fg, jnp.zeros((4, d), F32)], axis=0)
    full = lambda a: pl.BlockSpec(a.shape, lambda i: (0,) * a.ndim, pipeline_mode=pl.Buffered(1))
    tile_a = lambda i: jnp.minimum(i, n_tiles - 1)
    tile_c = lambda i: jnp.clip(i - 2, 0, n_tiles - 1)
    plane_blk = lambda tile: pl.BlockSpec((1, S5_R, jt, d), lambda i: (tile(i) // nt, 0, tile(i) % nt, 0))
    ubuf = pltpu.VMEM((n_strips, S5_R, CONV_CARRY + jt, LANES), F32)
    gate = pltpu.VMEM((width // MM_COLS, tm, MM_COLS), F32)
    return pl.pallas_call(
        functools.partial(_layer1_kernel, jt=jt, width=width, nt=nt),
        grid=(n_tiles + 2,),
        in_specs=[plane_blk(tile_a), plane_blk(tile_c),
                  full(vec), full(win3), full(cw3), full(wout3)],
        out_specs=pl.BlockSpec((tm, d), lambda i: (tile_c(i), 0)),
        out_shape=jax.ShapeDtypeStruct((nb * nj * S5_R, d), F32),
        scratch_shapes=[ubuf, ubuf, gate, gate,
                        pltpu.VMEM((n_strips, n_shift, S5_R, jt, LANES), F32),
                        pltpu.VMEM((n_strips, tm, LANES), F32),
                        _frame_scratch(d, tm),
                        pltpu.VMEM((tm, d), BF16),
                        pltpu.VMEM((3 * width // MM_COLS, tm, MM_COLS), F32),
                        pltpu.VMEM((tm, width), BF16),
                        pltpu.VMEM((d // MM_COLS, tm, MM_COLS), F32)],
        compiler_params=_cparams(("arbitrary",)),
        name="layer1",
    )(x3, x3, vec, win3, cw3, wout3)


def _pick_tile(n, pref):
    t = min(n, pref)
    while n % t:
        t //= 2
    return t


def kernel(x, norm_g, final_g, w_in_ab, s5_a_re, s5_a_im, s5_log_dt, s5_b_re, s5_b_im, s5_c_re, s5_c_im,
           s5_d, s5_glu_w, s5_glu_b, w_out_ab, w_in_c, conv_w, conv_b, conv_ln_g, conv_ln_b, w_out_c):
    nb, seq, d = x.shape
    rows = nb * seq
    s5_w = s5_glu_w.shape[-1]
    G = s5_w // S5_GROUP
    ret_qk = RET_HEADS * RET_DK
    ret_w = RET_HEADS * RET_DV
    nj = seq // S5_R
    x2 = x.reshape(rows, d)

    w_in = w_in_ab[0].astype(BF16)
    tm = _pick_tile(seq, 512)

    dup = lambda a: jnp.concatenate([a, a], axis=-1)
    are2 = dup(s5_a_re[0])[:, None, :]
    aim2 = dup(s5_a_im[0])[:, None, :]
    ldt = s5_log_dt[0][:, None, None]
    bre2 = dup(jnp.swapaxes(s5_b_re[0], 1, 2))
    bim2 = dup(jnp.swapaxes(s5_b_im[0], 1, 2))
    cre2 = dup(s5_c_re[0])
    cim2 = dup(s5_c_im[0])
    dsk_t = jnp.tile(s5_d[0].reshape(G, 1, S5_GROUP), (1, S5_R, 1)).reshape(G, S5_BLK, 1)
    mt, bsre, bsim, ccat, a16 = _s5_prep(are2, aim2, ldt, bre2, bim2, cre2, cim2, dsk_t)

    pos = jnp.arange(seq, dtype=F32)[:, None]
    half = RET_DK // 2
    freqs = ROPE_BASE ** (-jnp.arange(half, dtype=F32) / half)
    ang = pos * freqs[None, :]
    cos_t = jnp.tile(jnp.concatenate([jnp.cos(ang), jnp.cos(ang)], axis=-1), (1, LANES // RET_DK))
    sin_t = jnp.tile(jnp.concatenate([-jnp.sin(ang), jnp.sin(ang)], axis=-1), (1, LANES // RET_DK))
    log_g = jnp.log(1.0 - 2.0 ** (-5.0 - jnp.arange(RET_HEADS, dtype=F32)))
    idx = jnp.arange(RET_BLK, dtype=F32)
    chunk_of = jnp.arange(RET_BLK) // CHUNK
    seen = chunk_of[:, None] >= chunk_of[None, :]
    dm = jnp.where(seen[None], jnp.exp(log_g[:, None, None] * jnp.abs(idx[:, None] - idx[None, :])[None]), 0.0)
    qdec = jnp.exp(log_g[:, None] * (idx + 1.0)[None, :])[:, :, None]
    kdec = jnp.exp(log_g[:, None] * (RET_BLK - 1 - idx)[None, :])[:, :, None]
    cdec = jnp.broadcast_to(jnp.exp(log_g * RET_BLK)[:, None, None], (RET_HEADS, 1, RET_DV))
    yr3, gr3, hn3, x3 = _proj0_ret(x2, norm_g[0:1], w_in[:, 2 * s5_w:], cos_t, sin_t, dm, qdec, kdec, cdec,
                                   nb, seq, ret_qk, ret_w, tm)
    at, g5t = _proj_u(hn3, w_in[:, :2 * s5_w].T, s5_w, 4)
    y5t = _s5_core(at, mt, bsre, bsim, ccat, a16)

    wo = w_out_ab[0].astype(BF16)
    x3 = _merge0(x3, y5t, g5t, yr3, gr3, s5_glu_w[0].T.astype(BF16), s5_glu_b[0][:, None],
                 wo[:s5_w], wo[s5_w:])

    out = _layer1(x3, norm_g[1:2], w_in_c[0].astype(BF16), conv_w[0], conv_b[0:1], conv_ln_g[0:1],
                  conv_ln_b[0:1], w_out_c[0].astype(BF16), final_g[None, :], tm)
    return out.reshape(nb, seq, d)
```
